```python
import math
import jax, jax.numpy as jnp
from jax import lax
import numpy as np

D_MODEL = 2048
BATCH = 2
SEQ = 16384
DEPTH = 1

ATTN_HEADS = 8
ATTN_HEAD_DIM = 128
ATTN_WIDTH = ATTN_HEADS * ATTN_HEAD_DIM
Q_BLOCK = 128
SSM_WIDTH = D_MODEL // 4
SSM_GROUP = 16
SSM_GROUPS = SSM_WIDTH // SSM_GROUP
SSM_STATE = 64
DT_MIN = 1e-3
DT_MAX = 1e-1
N_BRANCHES = 2
PEER_HEADS = 8
PEER_N_KEYS = 128
PEER_N_EXPERTS = PEER_N_KEYS * PEER_N_KEYS
PEER_QUERY_DIM = 256
PEER_HALF_DIM = PEER_QUERY_DIM // 2
PEER_TOPK = 16
PEER_CHUNK = 128
N_MOD = 6
RMS_EPS = 1e-6
NEG_INF = -1e30

IN_SIZES = (ATTN_WIDTH, ATTN_WIDTH, ATTN_WIDTH, ATTN_HEADS, SSM_WIDTH, D_MODEL, D_MODEL)
IN_WIDTH = sum(IN_SIZES)
IN_SPLIT_POINTS = tuple(int(v) for v in np.cumsum(IN_SIZES)[:-1])

kernel_name = "hybrid_s5_fox_peer_block"


def rms_norm(x, w):
    xf = x.astype(jnp.float32)
    xf = xf * lax.rsqrt(jnp.mean(xf * xf, axis=-1, keepdims=True) + RMS_EPS)
    return xf.astype(x.dtype) * w


def modulate(h, shift, scale):
    return h * (1.0 + scale[:, None, :]) + shift[:, None, :]


def _ssm_combine(e_i, e_j):
    ar_i, ai_i, br_i, bi_i = e_i
    ar_j, ai_j, br_j, bi_j = e_j
    ar = ar_j * ar_i - ai_j * ai_i
    ai = ar_j * ai_i + ai_j * ar_i
    br = ar_j * br_i - ai_j * bi_i + br_j
    bi = ar_j * bi_i + ai_j * br_i + bi_j
    return (ar, ai, br, bi)


def s5_mixer(u, A_re, A_im, log_dt, B_re, B_im, C_re, C_im, D_skip):
    bsz, seq, _ = u.shape
    f32 = jnp.float32
    uf = u.astype(f32).reshape(bsz, seq, SSM_GROUPS, SSM_GROUP)
    lam_re = A_re.astype(f32)
    lam_im = A_im.astype(f32)
    dt = jnp.exp(log_dt.astype(f32))[:, None]
    mag = jnp.exp(lam_re * dt)
    ab_re = mag * jnp.cos(lam_im * dt)
    ab_im = mag * jnp.sin(lam_im * dt)
    den = lam_re * lam_re + lam_im * lam_im
    nr = ab_re - 1.0
    coef_re = (nr * lam_re + ab_im * lam_im) / den
    coef_im = (ab_im * lam_re - nr * lam_im) / den
    b_re = B_re.astype(f32)
    b_im = B_im.astype(f32)
    bb_re = coef_re[..., None] * b_re - coef_im[..., None] * b_im
    bb_im = coef_re[..., None] * b_im + coef_im[..., None] * b_re
    bu_re = jnp.einsum('bsgc,gpc->bsgp', uf, bb_re)
    bu_im = jnp.einsum('bsgc,gpc->bsgp', uf, bb_im)
    a_re = jnp.broadcast_to(ab_re, (1, seq, SSM_GROUPS, SSM_STATE))
    a_im = jnp.broadcast_to(ab_im, (1, seq, SSM_GROUPS, SSM_STATE))
    _, _, st_re, st_im = lax.associative_scan(_ssm_combine, (a_re, a_im, bu_re, bu_im), axis=1)
    y = (jnp.einsum('gcp,bsgp->bsgc', C_re.astype(f32), st_re)
         - jnp.einsum('gcp,bsgp->bsgc', C_im.astype(f32), st_im)
         + D_skip.astype(f32) * uf)
    return y.reshape(bsz, seq, SSM_WIDTH).astype(u.dtype)


def half_glu(y, w_glu, b_glu):
    g = jax.nn.gelu(y, approximate=False)
    return g * jax.nn.sigmoid(g @ w_glu + b_glu)


def head_rms(t, w):
    tf = t.astype(jnp.float32)
    tf = tf * lax.rsqrt(jnp.mean(tf * tf, axis=-1, keepdims=True) + RMS_EPS)
    return tf.astype(t.dtype) * w


def fox_attention(q, k, v, f_logit, b_forget, q_norm_w, k_norm_w):
    bsz, seq, _ = q.shape
    q = head_rms(q.reshape(bsz, seq, ATTN_HEADS, ATTN_HEAD_DIM), q_norm_w) * (ATTN_HEAD_DIM ** -0.5)
    k = head_rms(k.reshape(bsz, seq, ATTN_HEADS, ATTN_HEAD_DIM), k_norm_w)
    v = v.reshape(bsz, seq, ATTN_HEADS, ATTN_HEAD_DIM)
    log_f = jax.nn.log_sigmoid((f_logit + b_forget).astype(jnp.float32))
    cum = jnp.cumsum(log_f, axis=1).transpose(0, 2, 1)
    nb = seq // Q_BLOCK
    q_blocks = q.reshape(bsz, nb, Q_BLOCK, ATTN_HEADS, ATTN_HEAD_DIM).transpose(1, 0, 2, 3, 4)
    cum_blocks = cum.reshape(bsz, ATTN_HEADS, nb, Q_BLOCK).transpose(2, 0, 1, 3)
    kpos = jnp.arange(seq)

    def block(args):
        qb, cq, bi = args
        s = (jnp.einsum('bqhd,bkhd->bhqk', qb, k).astype(jnp.float32)
             + cq[..., None] - cum[:, :, None, :])
        qpos = bi * Q_BLOCK + jnp.arange(Q_BLOCK)
        s = jnp.where(kpos[None, :] <= qpos[:, None], s, NEG_INF)
        p = jax.nn.softmax(s, axis=-1).astype(v.dtype)
        return jnp.einsum('bhqk,bkhd->bqhd', p, v)

    o = lax.map(block, (q_blocks, cum_blocks, jnp.arange(nb)))
    return o.transpose(1, 0, 2, 3, 4).reshape(bsz, seq, ATTN_WIDTH)


def peer_ffn(h, w_q, sub_keys, expert_u, expert_v):
    bsz, seq, d = h.shape
    q = (h @ w_q).reshape(bsz, seq, PEER_HEADS, 2, PEER_HALF_DIM)
    scores = jnp.einsum('bshpd,hpnd->bshpn', q, sub_keys).astype(jnp.float32)
    sv, si = lax.top_k(scores, PEER_TOPK)
    cand = (sv[..., 0, :, None] + sv[..., 1, None, :]).reshape(bsz, seq, PEER_HEADS, PEER_TOPK * PEER_TOPK)
    cidx = (si[..., 0, :, None] * PEER_N_KEYS + si[..., 1, None, :]).reshape(bsz, seq, PEER_HEADS, PEER_TOPK * PEER_TOPK)
    tv, tp = lax.top_k(cand, PEER_TOPK)
    eidx = jnp.take_along_axis(cidx, tp, axis=-1)
    gw = jax.nn.softmax(tv, axis=-1).astype(h.dtype)
    n_sel = PEER_HEADS * PEER_TOPK
    nc = (bsz * seq) // PEER_CHUNK
    xs = h.reshape(nc, PEER_CHUNK, d)
    ids = eidx.reshape(nc, PEER_CHUNK, n_sel)
    ws = gw.reshape(nc, PEER_CHUNK, n_sel)

    def chunk(args):
        xc, ic, wc = args
        a = jnp.einsum('td,ted->te', xc, expert_u[ic])
        a = jax.nn.gelu(a, approximate=False) * wc
        return jnp.einsum('te,ted->td', a, expert_v[ic])

    out = lax.map(chunk, (xs, ids, ws))
    return out.reshape(bsz, seq, d)


def setup_inputs(seed: int = 0) -> dict:
    key = jax.random.key(seed)
    ks = jax.random.split(key, 32)
    f32 = jnp.float32
    L = DEPTH
    nrm = lambda k, shape, s: (jax.random.normal(k, shape, f32) * s)
    n_idx = jnp.arange(SSM_STATE, dtype=f32)
    A_re = -0.5 * jnp.exp(nrm(ks[9], (L, SSM_GROUPS, SSM_STATE), 0.05))
    A_im = math.pi * n_idx[None, None, :] + nrm(ks[10], (L, SSM_GROUPS, SSM_STATE), 0.01)
    log_dt = jax.random.uniform(ks[11], (L, SSM_GROUPS), f32, math.log(DT_MIN), math.log(DT_MAX))
    return {
        "x": nrm(ks[0], (BATCH, SEQ, D_MODEL), 1.0),
        "c": nrm(ks[1], (BATCH, D_MODEL), 1.0),
        "w_ada": nrm(ks[2], (L, D_MODEL, N_MOD * D_MODEL), 0.02),
        "b_ada": nrm(ks[3], (L, N_MOD * D_MODEL), 0.01),
        "norm1_w": 1.0 + nrm(ks[4], (L, D_MODEL), 0.02),
        "w_in": nrm(ks[5], (L, D_MODEL, IN_WIDTH), D_MODEL ** -0.5),
        "b_forget": jax.random.uniform(ks[6], (L, ATTN_HEADS), f32, 1.0, 3.0),
        "q_norm_w": 1.0 + nrm(ks[7], (L, ATTN_HEAD_DIM), 0.02),
        "k_norm_w": 1.0 + nrm(ks[8], (L, ATTN_HEAD_DIM), 0.02),
        "ssm_A_re": A_re,
        "ssm_A_im": A_im,
        "ssm_log_dt": log_dt,
        "ssm_B_re": nrm(ks[12], (L, SSM_GROUPS, SSM_STATE, SSM_GROUP), (2.0 * SSM_GROUP) ** -0.5),
        "ssm_B_im": nrm(ks[13], (L, SSM_GROUPS, SSM_STATE, SSM_GROUP), (2.0 * SSM_GROUP) ** -0.5),
        "ssm_C_re": nrm(ks[14], (L, SSM_GROUPS, SSM_GROUP, SSM_STATE), (2.0 * SSM_STATE) ** -0.5),
        "ssm_C_im": nrm(ks[15], (L, SSM_GROUPS, SSM_GROUP, SSM_STATE), (2.0 * SSM_STATE) ** -0.5),
        "ssm_D": 1.0 + nrm(ks[16], (L, SSM_GROUPS, SSM_GROUP), 0.1),
        "w_glu": nrm(ks[17], (L, SSM_WIDTH, SSM_WIDTH), SSM_WIDTH ** -0.5),
        "b_glu": nrm(ks[18], (L, SSM_WIDTH), 0.01),
        "w_ssm_up": nrm(ks[19], (L, SSM_WIDTH, D_MODEL), SSM_WIDTH ** -0.5),
        "w_attn_up": nrm(ks[20], (L, ATTN_WIDTH, D_MODEL), ATTN_WIDTH ** -0.5),
        "w_out": nrm(ks[21], (L, D_MODEL, D_MODEL), D_MODEL ** -0.5),
        "norm2_w": 1.0 + nrm(ks[22], (L, D_MODEL), 0.02),
        "w_peer_q": nrm(ks[23], (L, D_MODEL, PEER_HEADS * PEER_QUERY_DIM), D_MODEL ** -0.5),
        "peer_sub_keys": nrm(ks[24], (L, PEER_HEADS, 2, PEER_N_KEYS, PEER_HALF_DIM), PEER_HALF_DIM ** -0.5),
        "peer_u": nrm(ks[25], (L, PEER_N_EXPERTS, D_MODEL), D_MODEL ** -0.5),
        "peer_v": nrm(ks[26], (L, PEER_N_EXPERTS, D_MODEL), 0.25),
    }


def reference(x, c, w_ada, b_ada, norm1_w, w_in, b_forget, q_norm_w, k_norm_w,
              ssm_A_re, ssm_A_im, ssm_log_dt, ssm_B_re, ssm_B_im, ssm_C_re, ssm_C_im, ssm_D,
              w_glu, b_glu, w_ssm_up, w_attn_up, w_out, norm2_w, w_peer_q, peer_sub_keys,
              peer_u, peer_v):
    cond = jax.nn.silu(c)
    for l in range(DEPTH):
        mod = cond @ w_ada[l] + b_ada[l]
        sh1, sc1, g1, sh2, sc2, g2 = jnp.split(mod, N_MOD, axis=-1)
        h = modulate(rms_norm(x, norm1_w[l]), sh1, sc1)
        proj = h @ w_in[l]
        q, k, v, f_logit, u, gate_s, gate_a = jnp.split(proj, IN_SPLIT_POINTS, axis=-1)
        y_ssm = s5_mixer(u, ssm_A_re[l], ssm_A_im[l], ssm_log_dt[l], ssm_B_re[l], ssm_B_im[l],
                         ssm_C_re[l], ssm_C_im[l], ssm_D[l])
        y_ssm = half_glu(y_ssm, w_glu[l], b_glu[l])
        y_attn = fox_attention(q, k, v, f_logit, b_forget[l], q_norm_w[l], k_norm_w[l])
        merged = (jax.nn.sigmoid(gate_s) * (y_ssm @ w_ssm_up[l])
                  + jax.nn.sigmoid(gate_a) * (y_attn @ w_attn_up[l]))
        x = x + g1[:, None, :] * (merged @ w_out[l])
        h2 = modulate(rms_norm(x, norm2_w[l]), sh2, sc2)
        x = x + g2[:, None, :] * peer_ffn(h2, w_peer_q[l], peer_sub_keys[l], peer_u[l], peer_v[l])
    return x
```

```python
import functools
import math

import numpy as np
import jax
import jax.numpy as jnp
from jax import lax
from jax.experimental import pallas as pl
from jax.experimental.pallas import tpu as pltpu

F32 = jnp.float32
BF16 = jnp.bfloat16

RMS_EPS = 1e-6
MASK_VALUE = -1e30
N_MOD = 6
PEER_TOPK = 16
SQRT_HALF = math.sqrt(0.5)

LANES = 128
SUBLANES = 8
VMEM_LIMIT_BYTES = 56 * 1024 * 1024


def _params(*sem):
    return pltpu.CompilerParams(dimension_semantics=sem, vmem_limit_bytes=VMEM_LIMIT_BYTES)


def _gelu_exact(x):
    return 0.5 * x * (1.0 + lax.erf(x * SQRT_HALF))


def _adaln_kernel(c_ref, w_ref, b_ref, o_ref):
    c = c_ref[...]
    cond = c * jax.nn.sigmoid(c)
    o_ref[...] = jnp.dot(cond.astype(BF16), w_ref[...].astype(BF16), preferred_element_type=F32) + b_ref[...]


def _adaln(c, w, b, tn=512):
    bsz, d = c.shape
    n = w.shape[1]
    c8 = jnp.zeros((SUBLANES, d), F32).at[:bsz].set(c)
    out = pl.pallas_call(
        _adaln_kernel,
        grid=(n // tn,),
        in_specs=[pl.BlockSpec((SUBLANES, d), lambda j: (0, 0)),
                  pl.BlockSpec((d, tn), lambda j: (0, j)),
                  pl.BlockSpec((1, tn), lambda j: (0, j))],
        out_specs=pl.BlockSpec((SUBLANES, tn), lambda j: (0, j)),
        out_shape=jax.ShapeDtypeStruct((SUBLANES, n), F32),
        compiler_params=_params("arbitrary"),
        name="adaln",
    )(c8, w, b.reshape(1, n))
    return out[:bsz]


def _norm_mod(x, w, shift, scale):
    xf = x * lax.rsqrt(jnp.mean(x * x, axis=-1, keepdims=True) + RMS_EPS)
    return (xf * w) * (1.0 + scale) + shift


def _norm_mod_kernel(x_ref, w_ref, sh_ref, sc_ref, o_ref):
    o_ref[...] = _norm_mod(x_ref[...], w_ref[...], sh_ref[...], sc_ref[...]).astype(o_ref.dtype)


def _norm_modulate(x3, w, shift, scale, tm=512):
    bsz, seq, d = x3.shape
    tm = min(tm, seq)
    vec = pl.BlockSpec((None, 1, d), lambda b, i: (b, 0, 0))
    return pl.pallas_call(
        _norm_mod_kernel,
        grid=(bsz, seq // tm),
        in_specs=[pl.BlockSpec((None, tm, d), lambda b, i: (b, i, 0)),
                  pl.BlockSpec((1, d), lambda b, i: (0, 0)), vec, vec],
        out_specs=pl.BlockSpec((None, tm, d), lambda b, i: (b, i, 0)),
        out_shape=jax.ShapeDtypeStruct((bsz, seq, d), BF16),
        compiler_params=_params("arbitrary", "arbitrary"),
        name="norm_modulate",
    )(x3, w.reshape(1, d), shift.reshape(bsz, 1, d), scale.reshape(bsz, 1, d))


def _ep_plain(acc):
    return acc


def _ep_sigmoid(acc):
    return jax.nn.sigmoid(acc)


def _ep_head_rms(acc, w):
    outs = []
    for s in range(0, acc.shape[1], LANES):
        t = acc[:, s:s + LANES]
        t = t * lax.rsqrt(jnp.mean(t * t, axis=-1, keepdims=True) + RMS_EPS)
        outs.append(t * w[:, s:s + LANES])
    return jnp.concatenate(outs, axis=1)


def _ep_log_sigmoid(acc, b):
    z = acc + b
    return jnp.minimum(z, 0.0) - jnp.log1p(jnp.exp(-jnp.abs(z)))


def _proj_kernel(epilogue, a_ref, w_ref, *rest):
    o_ref = rest[-1]
    acc = jnp.dot(a_ref[...], w_ref[...], preferred_element_type=F32)
    o_ref[...] = epilogue(acc, *[r[...] for r in rest[:-1]]).astype(o_ref.dtype)


def _project(a, w, out_dtype, epilogue=_ep_plain, row_vec=None, tm=1024, tn=512, name="project"):
    m, k = a.shape
    n = w.shape[1]
    tm, tn = min(tm, m), min(tn, n)
    in_specs = [pl.BlockSpec((tm, k), lambda i, j: (i, 0)), pl.BlockSpec((k, tn), lambda i, j: (0, j))]
    args = [a, w]
    if row_vec is not None:
        in_specs.append(pl.BlockSpec((1, tn), lambda i, j: (0, j)))
        args.append(row_vec.reshape(1, n).astype(F32))
    return pl.pallas_call(
        functools.partial(_proj_kernel, epilogue),
        grid=(m // tm, n // tn),
        in_specs=in_specs,
        out_specs=pl.BlockSpec((tm, tn), lambda i, j: (i, j)),
        out_shape=jax.ShapeDtypeStruct((m, n), out_dtype),
        compiler_params=_params("arbitrary", "arbitrary"),
        name=name,
    )(*args)


def _cumsum_kernel(x_ref, o_ref):
    x = x_ref[...]
    n = x.shape[1]
    lane = lax.broadcasted_iota(jnp.int32, x.shape, 1)
    k = 1
    while k < n:
        x = x + jnp.where(lane >= k, pltpu.roll(x, k, axis=1), 0.0)
        k *= 2
    o_ref[...] = x


def _cumsum_lanes(x):
    bsz, r, n = x.shape
    return pl.pallas_call(
        _cumsum_kernel,
        grid=(bsz,),
        in_specs=[pl.BlockSpec((None, r, n), lambda b: (b, 0, 0))],
        out_specs=pl.BlockSpec((None, r, n), lambda b: (b, 0, 0)),
        out_shape=jax.ShapeDtypeStruct((bsz, r, n), F32),
        compiler_params=_params("arbitrary"),
        name="forget_cumsum",
    )(x)


def _attn_kernel(n_heads, qi_ref, kj_ref, q_ref, k_ref, v_ref, cq_ref, ck_ref, o_ref, m_scr, l_scr, acc_scr):
    step = pl.program_id(1)
    i = qi_ref[step]
    j = kj_ref[step]
    tq = q_ref.shape[0]
    tk = k_ref.shape[0]

    @pl.when(j == 0)
    def _():
        m_scr[...] = jnp.full(m_scr.shape, MASK_VALUE, F32)
        l_scr[...] = jnp.zeros(l_scr.shape, F32)
        acc_scr[...] = jnp.zeros(acc_scr.shape, F32)

    def update(masked):
        if masked:
            row = lax.broadcasted_iota(jnp.int32, (tq, tk), 0)
            col = lax.broadcasted_iota(jnp.int32, (tq, tk), 1)
            keep = col <= row
        for h in range(n_heads):
            sl = slice(h * LANES, (h + 1) * LANES)
            s = lax.dot_general(q_ref[:, sl], k_ref[:, sl], (((1,), (1,)), ((), ())),
                                preferred_element_type=F32)
            s = s + (cq_ref[h:h + 1, 0:1] - ck_ref[h:h + 1, :])
            if masked:
                s = jnp.where(keep, s, MASK_VALUE)
            m_prev = m_scr[h]
            m_new = jnp.maximum(m_prev, jnp.max(s, axis=1, keepdims=True))
            alpha = jnp.exp(m_prev - m_new)
            p = jnp.exp(s - m_new)
            l_scr[h] = alpha * l_scr[h] + jnp.sum(p, axis=1, keepdims=True)
            acc_scr[h] = alpha * acc_scr[h] + jnp.dot(p.astype(BF16), v_ref[:, sl], preferred_element_type=F32)
            m_scr[h] = m_new

    @pl.when(j < i)
    def _():
        update(False)

    @pl.when(j == i)
    def _():
        update(True)
        for h in range(n_heads):
            o_ref[:, h * LANES:(h + 1) * LANES] = (acc_scr[h] / l_scr[h]).astype(o_ref.dtype)


def _fox_attention(qk, v, cum, n_heads, bsz, seq, t=512):
    t = min(t, seq)
    nt = seq // t
    width = n_heads * LANES
    pairs = [(i, j) for i in range(nt) for j in range(i + 1)]
    qi = jnp.asarray(np.array([p[0] for p in pairs], np.int32))
    kj = jnp.asarray(np.array([p[1] for p in pairs], np.int32))
    cum_t = cum.reshape(bsz, cum.shape[1], nt, t).transpose(0, 2, 1, 3)
    grid_spec = pltpu.PrefetchScalarGridSpec(
        num_scalar_prefetch=2,
        grid=(bsz, len(pairs)),
        in_specs=[
            pl.BlockSpec((t, width), lambda b, s, qi, kj: (b * nt + qi[s], 0)),
            pl.BlockSpec((t, width), lambda b, s, qi, kj: (b * nt + kj[s], 1)),
            pl.BlockSpec((t, width), lambda b, s, qi, kj: (b * nt + kj[s], 0)),
            pl.BlockSpec((None, None, cum.shape[1], t), lambda b, s, qi, kj: (b, qi[s], 0, 0)),
            pl.BlockSpec((None, None, cum.shape[1], t), lambda b, s, qi, kj: (b, kj[s], 0, 0)),
        ],
        out_specs=pl.BlockSpec((t, width), lambda b, s, qi, kj: (b * nt + qi[s], 0)),
        scratch_shapes=[pltpu.VMEM((n_heads, t, 1), F32), pltpu.VMEM((n_heads, t, 1), F32),
                        pltpu.VMEM((n_heads, t, LANES), F32)],
    )
    return pl.pallas_call(
        functools.partial(_attn_kernel, n_heads),
        grid_spec=grid_spec,
        out_shape=jax.ShapeDtypeStruct((bsz * seq, width), BF16),
        compiler_params=_params("arbitrary", "arbitrary"),
        name="fox_attention",
    )(qi, kj, qk, qk, v, cum_t, cum_t)


def _ssm_prep_kernel(are_ref, aim_ref, ldt_ref, bre_ref, bim_ref, cre_ref, cim_ref, bbre_ref, bbim_ref):
    lam_re = are_ref[...]
    lam_im = aim_ref[...]
    dt = jnp.exp(ldt_ref[...])
    mag = jnp.exp(lam_re * dt)
    ab_re = mag * jnp.cos(lam_im * dt)
    ab_im = mag * jnp.sin(lam_im * dt)
    den = lam_re * lam_re + lam_im * lam_im
    nr = ab_re - 1.0
    coef_re = (nr * lam_re + ab_im * lam_im) / den
    coef_im = (ab_im * lam_re - nr * lam_im) / den
    b_re = bre_ref[...]
    b_im = bim_ref[...]
    bbre_ref[...] = coef_re * b_re - coef_im * b_im
    bbim_ref[...] = coef_re * b_im + coef_im * b_re
    pr, pi = ab_re, ab_im
    pows = []
    for _ in range(SUBLANES):
        pows.append((pr, pi))
        pr, pi = pr * ab_re - pi * ab_im, pr * ab_im + pi * ab_re
    row = lax.broadcasted_iota(jnp.int32, (SUBLANES, lam_re.shape[1]), 0)
    for idx, k in enumerate((1, 2, 4)):
        cre_ref[idx] = jnp.where(row >= k, pows[k - 1][0], 0.0)
        cim_ref[idx] = jnp.where(row >= k, pows[k - 1][1], 0.0)
    cre_ref[3] = jnp.concatenate([p[0] for p in pows], axis=0)
    cim_ref[3] = jnp.concatenate([p[1] for p in pows], axis=0)


def _ssm_prep(a_re, a_im, log_dt, b_re, b_im):
    g, p = a_re.shape
    gc = b_re.shape[2]
    gp = g * p
    flat = lambda a: a.reshape(1, gp)
    ldt = jnp.broadcast_to(log_dt[:, None], (g, p)).reshape(1, gp)
    bt = lambda b: b.transpose(2, 0, 1).reshape(gc, gp)
    return pl.pallas_call(
        _ssm_prep_kernel,
        out_shape=[jax.ShapeDtypeStruct((4, SUBLANES, gp), F32)] * 2 + [jax.ShapeDtypeStruct((gc, gp), F32)] * 2,
        name="ssm_discretize",
    )(flat(a_re), flat(a_im), ldt, bt(b_re), bt(b_im))


def _ssm_kernel(lane_chunk, u_ref, bre_ref, bim_ref, cfre_ref, cfim_ref, ctre_ref, ctim_ref, d_ref,
                wg_ref, bg_ref, o_ref, xre_scr, xim_scr, car_scr, cai_scr):
    tt = u_ref.shape[0]
    gp = xre_scr.shape[1]

    @pl.when(pl.program_id(1) == 0)
    def _():
        car_scr[...] = jnp.zeros(car_scr.shape, F32)
        cai_scr[...] = jnp.zeros(cai_scr.shape, F32)

    u = u_ref[...]
    ub = u.astype(BF16)
    xre_scr[...] = jnp.dot(ub, bre_ref[...], preferred_element_type=F32)
    xim_scr[...] = jnp.dot(ub, bim_ref[...], preferred_element_type=F32)

    for c0 in range(0, gp, lane_chunk):
        cols = slice(c0, c0 + lane_chunk)

        def body(r, carry, cols=cols):
            cr, ci = carry
            coefs = [(cfre_ref[n, :, cols], cfim_ref[n, :, cols]) for n in range(4)]
            rows = pl.ds(pl.multiple_of(r * SUBLANES, SUBLANES), SUBLANES)
            xr = xre_scr[rows, cols]
            xi = xim_scr[rows, cols]
            for n, k in enumerate((1, 2, 4)):
                ar, ai = coefs[n]
                sr = pltpu.roll(xr, k, axis=0)
                si = pltpu.roll(xi, k, axis=0)
                xr, xi = xr + ar * sr - ai * si, xi + ar * si + ai * sr
            pr, pi = coefs[3]
            xr, xi = xr + pr * cr - pi * ci, xi + pr * ci + pi * cr
            xre_scr[rows, cols] = xr
            xim_scr[rows, cols] = xi
            last = slice(SUBLANES - 1, SUBLANES)
            return (jnp.broadcast_to(xr[last, :], xr.shape), jnp.broadcast_to(xi[last, :], xi.shape))

        cr, ci = lax.fori_loop(0, tt // SUBLANES, body, (car_scr[:, cols], cai_scr[:, cols]))
        car_scr[:, cols] = cr
        cai_scr[:, cols] = ci

    y = (jnp.dot(xre_scr[...].astype(BF16), ctre_ref[...], preferred_element_type=F32)
         - jnp.dot(xim_scr[...].astype(BF16), ctim_ref[...], preferred_element_type=F32)
         + d_ref[...] * u)
    g = _gelu_exact(y)
    gate = jnp.dot(g.astype(BF16), wg_ref[...], preferred_element_type=F32) + bg_ref[...]
    o_ref[...] = (g * jax.nn.sigmoid(gate)).astype(o_ref.dtype)


def _block_diag_rows(m, groups):
    gc, gp = m.shape
    p = gp // groups
    rows = jnp.tile(m, (groups, 1))
    rg = lax.broadcasted_iota(jnp.int32, (groups * gc, gp), 0) // gc
    cg = lax.broadcasted_iota(jnp.int32, (groups * gc, gp), 1) // p
    return jnp.where(rg == cg, rows, 0.0)


def _s5_glu(u3, coef_re, coef_im, bb_re, bb_im, c_re, c_im, d_skip, w_glu, b_glu, tt=512, lane_chunk=512):
    bsz, seq, width = u3.shape
    g, gc, p = c_re.shape
    gp = g * p
    tt = min(tt, seq)
    lane_chunk = min(lane_chunk, gp)
    bd_b_re = _block_diag_rows(bb_re, g).astype(BF16)
    bd_b_im = _block_diag_rows(bb_im, g).astype(BF16)
    ct = lambda c: _block_diag_rows(c.transpose(1, 0, 2).reshape(gc, gp), g).T.astype(BF16)
    const = lambda shape: pl.BlockSpec(shape, lambda b, i: tuple(0 for _ in shape))
    return pl.pallas_call(
        functools.partial(_ssm_kernel, lane_chunk),
        grid=(bsz, seq // tt),
        in_specs=[pl.BlockSpec((None, tt, width), lambda b, i: (b, i, 0)),
                  const((width, gp)), const((width, gp)),
                  const((4, SUBLANES, gp)), const((4, SUBLANES, gp)),
                  const((gp, width)), const((gp, width)),
                  const((1, width)), const((width, width)), const((1, width))],
        out_specs=pl.BlockSpec((None, tt, width), lambda b, i: (b, i, 0)),
        out_shape=jax.ShapeDtypeStruct((bsz, seq, width), BF16),
        scratch_shapes=[pltpu.VMEM((tt, gp), F32), pltpu.VMEM((tt, gp), F32),
                        pltpu.VMEM((SUBLANES, gp), F32), pltpu.VMEM((SUBLANES, gp), F32)],
        compiler_params=_params("arbitrary", "arbitrary"),
        name="s5_glu",
    )(u3, bd_b_re, bd_b_im, coef_re, coef_im, ct(c_re), ct(c_im),
      d_skip.reshape(1, width).astype(F32), w_glu.astype(BF16), b_glu.reshape(1, width).astype(F32))


def _merge_kernel(ys_ref, ya_ref, gt_ref, wsu_ref, wau_ref, o_ref):
    d = o_ref.shape[1]
    ms = jnp.dot(ys_ref[...], wsu_ref[...], preferred_element_type=F32)
    ma = jnp.dot(ya_ref[...], wau_ref[...], preferred_element_type=F32)
    o_ref[...] = (gt_ref[:, :d].astype(F32) * ms + gt_ref[:, d:].astype(F32) * ma).astype(o_ref.dtype)


def _merge(ys, ya, gates, w_ssm_up, w_attn_up, tm=512):
    m = ys.shape[0]
    d = w_ssm_up.shape[1]
    tm = min(tm, m)
    row = lambda w: pl.BlockSpec((tm, w), lambda i: (i, 0))
    const = lambda a: pl.BlockSpec(a.shape, lambda i: (0, 0))
    return pl.pallas_call(
        _merge_kernel,
        grid=(m // tm,),
        in_specs=[row(ys.shape[1]), row(ya.shape[1]), row(gates.shape[1]), const(w_ssm_up), const(w_attn_up)],
        out_specs=row(d),
        out_shape=jax.ShapeDtypeStruct((m, d), BF16),
        compiler_params=_params("arbitrary"),
        name="gated_merge",
    )(ys, ya, gates, w_ssm_up, w_attn_up)


def _out_proj_kernel(mg_ref, w_ref, x_ref, g1_ref, nw_ref, sh_ref, sc_ref, x1_ref, h2_ref):
    x1 = x_ref[...] + g1_ref[...] * jnp.dot(mg_ref[...], w_ref[...], preferred_element_type=F32)
    x1_ref[...] = x1
    h2_ref[...] = _norm_mod(x1, nw_ref[...], sh_ref[...], sc_ref[...]).astype(h2_ref.dtype)


def _out_proj(merged3, w_out, x3, g1, norm_w, shift, scale, tm=512):
    bsz, seq, d = x3.shape
    tm = min(tm, seq)
    row = pl.BlockSpec((None, tm, d), lambda b, i: (b, i, 0))
    vec = pl.BlockSpec((None, 1, d), lambda b, i: (b, 0, 0))
    r3 = lambda a: a.reshape(bsz, 1, d)
    return pl.pallas_call(
        _out_proj_kernel,
        grid=(bsz, seq // tm),
        in_specs=[row, pl.BlockSpec((d, d), lambda b, i: (0, 0)), row, vec,
                  pl.BlockSpec((1, d), lambda b, i: (0, 0)), vec, vec],
        out_specs=[row, row],
        out_shape=[jax.ShapeDtypeStruct((bsz, seq, d), F32), jax.ShapeDtypeStruct((bsz, seq, d), BF16)],
        compiler_params=_params("arbitrary", "arbitrary"),
        name="out_proj_residual",
    )(merged3, w_out, x3, r3(g1), norm_w.reshape(1, d), r3(shift), r3(scale))


def _peer_scores_kernel(h_ref, wq_ref, keys_ref, o_ref):
    q = jnp.dot(h_ref[...], wq_ref[...], preferred_element_type=F32).astype(BF16)
    n_hp = keys_ref.shape[0]
    nc = o_ref.shape[1]
    for hp in range(n_hp):
        dk = keys_ref.shape[2]
        st = lax.dot_general(keys_ref[hp], q[:, hp * dk:(hp + 1) * dk], (((1,), (1,)), ((), ())),
                             preferred_element_type=F32)
        for c in range(nc):
            o_ref[hp, c] = st[:, c * LANES:(c + 1) * LANES]


def _peer_scores(h2, w_q, keys, tm=512):
    m, d = h2.shape
    n_hp, n_keys, dk = keys.shape
    tm = min(tm, m)
    nc = tm // LANES
    return pl.pallas_call(
        _peer_scores_kernel,
        grid=(m // tm,),
        in_specs=[pl.BlockSpec((tm, d), lambda i: (i, 0)),
                  pl.BlockSpec(w_q.shape, lambda i: (0, 0)),
                  pl.BlockSpec(keys.shape, lambda i: (0, 0, 0))],
        out_specs=pl.BlockSpec((n_hp, nc, n_keys, LANES), lambda i: (0, i, 0, 0)),
        out_shape=jax.ShapeDtypeStruct((n_hp, m // LANES, n_keys, LANES), F32),
        compiler_params=_params("arbitrary"),
        name="peer_scores",
    )(h2, w_q, keys)


def _candidate_pairs(k):
    return [(x, y) for x in range(k) for y in range(k) if (x + 1) * (y + 1) <= k]


def _top_values(s, k):
    vals, cnts = [], []
    for _ in range(k):
        m = jnp.max(s, axis=0, keepdims=True)
        eq = s == m
        cnts.append(jnp.sum(jnp.where(eq, 1.0, 0.0), axis=0, keepdims=True))
        vals.append(m)
        s = jnp.where(eq, -jnp.inf, s)
    return vals, cnts


def _peer_route(sc_ref, t1_scr, e1_scr, e2_scr, cand_scr, mult_scr, n_heads, nc):
    k = PEER_TOPK
    pairs = _candidate_pairs(k)
    n_rows = cand_scr.shape[0]

    def body(idx, carry):
        h = idx // nc
        c = idx % nc
        s1 = sc_ref[2 * h, c]
        s2 = sc_ref[2 * h + 1, c]
        a, ma = _top_values(s1, k)
        b, mb = _top_values(s2, k)
        cand_scr[...] = jnp.full(cand_scr.shape, -jnp.inf, F32)
        mult_scr[...] = jnp.zeros(mult_scr.shape, F32)
        for r, (x, y) in enumerate(pairs):
            cand_scr[r:r + 1, :] = a[x] + b[y]
            mult_scr[r:r + 1, :] = ma[x] * mb[y]
        cand = cand_scr[...]
        mult = mult_scr[...]
        work = cand
        remaining = jnp.full((1, LANES), float(k), F32)
        tau = jnp.full((1, LANES), -jnp.inf, F32)
        for _ in range(k):
            m = jnp.max(work, axis=0, keepdims=True)
            eq = work == m
            cnt = jnp.sum(jnp.where(eq, mult, 0.0), axis=0, keepdims=True)
            after = remaining - cnt
            tau = jnp.where((remaining > 0.0) & (after <= 0.0), m, tau)
            remaining = after
            work = jnp.where(eq, -jnp.inf, work)
        top = a[0] + b[0]
        z = jnp.sum(jnp.where(cand >= tau, jnp.exp(cand - top) * mult, 0.0), axis=0, keepdims=True)
        t1 = jnp.full(s1.shape, jnp.inf, F32)
        for y in range(k):
            t1 = jnp.where(s1 + b[y] >= tau, b[y], t1)
        t1_scr[h, c] = t1
        e1_scr[h, c] = jnp.exp(s1 - a[0]) / z
        e2_scr[h, c] = jnp.exp(s2 - b[0])
        return carry

    lax.fori_loop(0, n_heads * nc, body, 0)
    del n_rows


def _peer_dense_kernel(n_heads, h_ref, u_ref, v_ref, sc_ref, o_ref,
                       t1_scr, e1_scr, e2_scr, cand_scr, mult_scr, acc_scr, g_scr):
    e = pl.program_id(1)
    et, tm = g_scr.shape
    nc = tm // LANES
    n_keys = sc_ref.shape[2]

    @pl.when(e == 0)
    def _():
        _peer_route(sc_ref, t1_scr, e1_scr, e2_scr, cand_scr, mult_scr, n_heads, nc)
        acc_scr[...] = jnp.zeros(acc_scr.shape, F32)

    at = lax.dot_general(u_ref[...], h_ref[...], (((1,), (1,)), ((), ())), preferred_element_type=F32)
    for ib in range(et // n_keys):
        i = e * (et // n_keys) + ib
        for c in range(nc):
            w = jnp.zeros((n_keys, LANES), F32)
            for h in range(n_heads):
                t1 = t1_scr[h, c, pl.ds(i, 1), :]
                e1 = e1_scr[h, c, pl.ds(i, 1), :]
                w = w + jnp.where(sc_ref[2 * h + 1, c] >= t1, e2_scr[h, c], 0.0) * e1
            a_blk = at[ib * n_keys:(ib + 1) * n_keys, c * LANES:(c + 1) * LANES]
            g_scr[ib * n_keys:(ib + 1) * n_keys, c * LANES:(c + 1) * LANES] = (_gelu_exact(a_blk) * w).astype(BF16)
    acc_scr[...] += lax.dot_general(g_scr[...], v_ref[...], (((0,), (0,)), ((), ())), preferred_element_type=F32)

    @pl.when(e == pl.num_programs(1) - 1)
    def _():
        o_ref[...] = acc_scr[...]


def _peer_dense(h2, u_tab, v_tab, scores, n_heads, tm=512, et=512):
    m, d = h2.shape
    n_exp = u_tab.shape[0]
    n_hp, _, n_keys, _ = scores.shape
    tm = min(tm, m)
    nc = tm // LANES
    n_cand = -(-len(_candidate_pairs(PEER_TOPK)) // SUBLANES) * SUBLANES
    per_tok = pltpu.VMEM((n_heads, nc, n_keys, LANES), F32)
    return pl.pallas_call(
        functools.partial(_peer_dense_kernel, n_heads),
        grid=(m // tm, n_exp // et),
        in_specs=[pl.BlockSpec((tm, d), lambda i, e: (i, 0)),
                  pl.BlockSpec((et, d), lambda i, e: (e, 0)),
                  pl.BlockSpec((et, d), lambda i, e: (e, 0)),
                  pl.BlockSpec((n_hp, nc, n_keys, LANES), lambda i, e: (0, i, 0, 0))],
        out_specs=pl.BlockSpec((tm, d), lambda i, e: (i, 0)),
        out_shape=jax.ShapeDtypeStruct((m, d), F32),
        scratch_shapes=[per_tok, per_tok, per_tok,
                        pltpu.VMEM((n_cand, LANES), F32), pltpu.VMEM((n_cand, LANES), F32),
                        pltpu.VMEM((tm, d), F32), pltpu.VMEM((et, tm), BF16)],
        compiler_params=_params("arbitrary", "arbitrary"),
        name="peer_dense",
    )(h2, u_tab, v_tab, scores)


def _final_kernel(x_ref, g_ref, p_ref, o_ref):
    o_ref[...] = x_ref[...] + g_ref[...] * p_ref[...]


def _final_residual(x3, g2, peer3, tm=512):
    bsz, seq, d = x3.shape
    tm = min(tm, seq)
    row = pl.BlockSpec((None, tm, d), lambda b, i: (b, i, 0))
    return pl.pallas_call(
        _final_kernel,
        grid=(bsz, seq // tm),
        in_specs=[row, pl.BlockSpec((None, 1, d), lambda b, i: (b, 0, 0)), row],
        out_specs=row,
        out_shape=jax.ShapeDtypeStruct((bsz, seq, d), F32),
        compiler_params=_params("arbitrary", "arbitrary"),
        name="final_residual",
    )(x3, g2.reshape(bsz, 1, d), peer3)


def _layer(x, cond_in, l, w_ada, b_ada, norm1_w, w_in, b_forget, q_norm_w, k_norm_w,
           ssm_A_re, ssm_A_im, ssm_log_dt, ssm_B_re, ssm_B_im, ssm_C_re, ssm_C_im, ssm_D,
           w_glu, b_glu, w_ssm_up, w_attn_up, w_out, norm2_w, w_peer_q, peer_sub_keys, peer_u, peer_v):
    bsz, seq, d = x.shape
    tokens = bsz * seq
    n_heads = b_forget.shape[1]
    head_dim = q_norm_w.shape[1]
    assert head_dim == LANES
    attn_w = n_heads * head_dim
    ssm_w = w_glu.shape[1]
    peer_heads = peer_sub_keys.shape[1]

    mod = _adaln(cond_in, w_ada[l], b_ada[l])
    sh1, sc1, g1, sh2, sc2, g2 = jnp.split(mod, N_MOD, axis=-1)

    h = _norm_modulate(x, norm1_w[l], sh1, sc1).reshape(tokens, d)
    wi = w_in[l].astype(BF16)
    o_qk, o_v, o_f, o_u, o_g = 0, 2 * attn_w, 3 * attn_w, 3 * attn_w + n_heads, 3 * attn_w + n_heads + ssm_w
    qk_w = jnp.concatenate([jnp.tile(q_norm_w[l] * head_dim ** -0.5, n_heads), jnp.tile(k_norm_w[l], n_heads)])
    qk = _project(h, wi[:, o_qk:o_v], BF16, _ep_head_rms, qk_w, name="proj_qk")
    v = _project(h, wi[:, o_v:o_f], BF16, name="proj_v")
    u = _project(h, wi[:, o_u:o_g], F32, name="proj_u")
    gates = _project(h, wi[:, o_g:], BF16, _ep_sigmoid, name="proj_gates")
    w_f = jnp.zeros((d, LANES), BF16).at[:, :n_heads].set(wi[:, o_f:o_u])
    b_f = jnp.zeros((LANES,), F32).at[:n_heads].set(b_forget[l])
    log_f = _project(h, w_f, F32, _ep_log_sigmoid, b_f, name="proj_forget")
    log_f_t = log_f.reshape(bsz, seq, LANES)[:, :, :SUBLANES].transpose(0, 2, 1)
    cum = _cumsum_lanes(log_f_t)
    y_attn = _fox_attention(qk, v, cum, n_heads, bsz, seq)

    coef_re, coef_im, bb_re, bb_im = _ssm_prep(ssm_A_re[l], ssm_A_im[l], ssm_log_dt[l], ssm_B_re[l], ssm_B_im[l])
    y_ssm = _s5_glu(u.reshape(bsz, seq, ssm_w), coef_re, coef_im, bb_re, bb_im, ssm_C_re[l], ssm_C_im[l],
                    ssm_D[l], w_glu[l], b_glu[l]).reshape(tokens, ssm_w)

    merged = _merge(y_ssm, y_attn, gates, w_ssm_up[l].astype(BF16), w_attn_up[l].astype(BF16))
    x1, h2 = _out_proj(merged.reshape(bsz, seq, d), w_out[l].astype(BF16), x, g1, norm2_w[l], sh2, sc2)

    keys = peer_sub_keys[l].reshape(2 * peer_heads, peer_sub_keys.shape[3], peer_sub_keys.shape[4]).astype(BF16)
    h2f = h2.reshape(tokens, d)
    scores = _peer_scores(h2f, w_peer_q[l].astype(BF16), keys)
    peer = _peer_dense(h2f, peer_u[l].astype(BF16), peer_v[l].astype(BF16), scores, peer_heads)
    return _final_residual(x1, g2, peer.reshape(bsz, seq, d))


def kernel(x, c, w_ada, b_ada, norm1_w, w_in, b_forget, q_norm_w, k_norm_w, ssm_A_re, ssm_A_im, ssm_log_dt,
           ssm_B_re, ssm_B_im, ssm_C_re, ssm_C_im, ssm_D, w_glu, b_glu, w_ssm_up, w_attn_up, w_out, norm2_w,
           w_peer_q, peer_sub_keys, peer_u, peer_v):
    for l in range(w_ada.shape[0]):
        x = _layer(x, c, l, w_ada, b_ada, norm1_w, w_in, b_forget, q_norm_w, k_norm_w,
                   ssm_A_re, ssm_A_im, ssm_log_dt, ssm_B_re, ssm_B_im, ssm_C_re, ssm_C_im, ssm_D,
                   w_glu, b_glu, w_ssm_up, w_attn_up, w_out, norm2_w, w_peer_q, peer_sub_keys, peer_u, peer_v)
    return x
```

```python
import functools
import math

import numpy as np
import jax
import jax.numpy as jnp
from jax import lax
from jax.experimental import pallas as pl
from jax.experimental.pallas import tpu as pltpu

F32 = jnp.float32
BF16 = jnp.bfloat16

RMS_EPS = 1e-6
MASK_VALUE = -1e30
N_MOD = 6
PEER_TOPK = 16
SQRT_HALF = math.sqrt(0.5)
LOG2E = math.log2(math.e)
BIAS_TERMS = 3

LANES = 128
SUBLANES = 8
VMEM_LIMIT_BYTES = 56 * 1024 * 1024


def _params(*sem):
    return pltpu.CompilerParams(dimension_semantics=sem, vmem_limit_bytes=VMEM_LIMIT_BYTES)


def _gelu_exact(x):
    return 0.5 * x * (1.0 + lax.erf(x * SQRT_HALF))


def _adaln_kernel(c_ref, w_ref, b_ref, o_ref):
    c = c_ref[...]
    cond = c * jax.nn.sigmoid(c)
    o_ref[...] = jnp.dot(cond.astype(BF16), w_ref[...].astype(BF16), preferred_element_type=F32) + b_ref[...]


def _adaln(c, w, b, tn=512):
    bsz, d = c.shape
    n = w.shape[1]
    c8 = jnp.zeros((SUBLANES, d), F32).at[:bsz].set(c)
    out = pl.pallas_call(
        _adaln_kernel,
        grid=(n // tn,),
        in_specs=[pl.BlockSpec((SUBLANES, d), lambda j: (0, 0)),
                  pl.BlockSpec((d, tn), lambda j: (0, j)),
                  pl.BlockSpec((1, tn), lambda j: (0, j))],
        out_specs=pl.BlockSpec((SUBLANES, tn), lambda j: (0, j)),
        out_shape=jax.ShapeDtypeStruct((SUBLANES, n), F32),
        compiler_params=_params("arbitrary"),
        name="adaln",
    )(c8, w, b.reshape(1, n))
    return out[:bsz]


def _norm_mod(x, w, shift, scale):
    xf = x * lax.rsqrt(jnp.mean(x * x, axis=-1, keepdims=True) + RMS_EPS)
    return (xf * w) * (1.0 + scale) + shift


def _norm_mod_kernel(x_ref, w_ref, sh_ref, sc_ref, o_ref):
    o_ref[...] = _norm_mod(x_ref[...], w_ref[...], sh_ref[...], sc_ref[...]).astype(o_ref.dtype)


def _norm_modulate(x3, w, shift, scale, tm=512):
    bsz, seq, d = x3.shape
    tm = min(tm, seq)
    vec = pl.BlockSpec((None, 1, d), lambda b, i: (b, 0, 0))
    return pl.pallas_call(
        _norm_mod_kernel,
        grid=(bsz, seq // tm),
        in_specs=[pl.BlockSpec((None, tm, d), lambda b, i: (b, i, 0)),
                  pl.BlockSpec((1, d), lambda b, i: (0, 0)), vec, vec],
        out_specs=pl.BlockSpec((None, tm, d), lambda b, i: (b, i, 0)),
        out_shape=jax.ShapeDtypeStruct((bsz, seq, d), BF16),
        compiler_params=_params("arbitrary", "arbitrary"),
        name="norm_modulate",
    )(x3, w.reshape(1, d), shift.reshape(bsz, 1, d), scale.reshape(bsz, 1, d))


def _ep_plain(acc):
    return acc


def _ep_sigmoid(acc):
    return jax.nn.sigmoid(acc)


def _ep_head_rms(acc, w):
    outs = []
    for s in range(0, acc.shape[1], LANES):
        t = acc[:, s:s + LANES]
        t = t * lax.rsqrt(jnp.mean(t * t, axis=-1, keepdims=True) + RMS_EPS)
        outs.append(t * w[:, s:s + LANES])
    return jnp.concatenate(outs, axis=1)


def _ep_log_sigmoid(acc, b):
    z = acc + b
    return jnp.minimum(z, 0.0) - jnp.log1p(jnp.exp(-jnp.abs(z)))


def _proj_kernel(epilogue, a_ref, w_ref, *rest):
    o_ref = rest[-1]
    acc = jnp.dot(a_ref[...], w_ref[...], preferred_element_type=F32)
    o_ref[...] = epilogue(acc, *[r[...] for r in rest[:-1]]).astype(o_ref.dtype)


def _project(a, w, out_dtype, epilogue=_ep_plain, row_vec=None, tm=1024, tn=512, name="project"):
    m, k = a.shape
    n = w.shape[1]
    tm, tn = min(tm, m), min(tn, n)
    in_specs = [pl.BlockSpec((tm, k), lambda i, j: (i, 0)), pl.BlockSpec((k, tn), lambda i, j: (0, j))]
    args = [a, w]
    if row_vec is not None:
        in_specs.append(pl.BlockSpec((1, tn), lambda i, j: (0, j)))
        args.append(row_vec.reshape(1, n).astype(F32))
    return pl.pallas_call(
        functools.partial(_proj_kernel, epilogue),
        grid=(m // tm, n // tn),
        in_specs=in_specs,
        out_specs=pl.BlockSpec((tm, tn), lambda i, j: (i, j)),
        out_shape=jax.ShapeDtypeStruct((m, n), out_dtype),
        compiler_params=_params("arbitrary", "arbitrary"),
        name=name,
    )(*args)


def _proj_t_kernel(a_ref, wt_ref, o_ref):
    o_ref[...] = lax.dot_general(wt_ref[...], a_ref[...], (((1,), (1,)), ((), ())),
                                 preferred_element_type=F32).astype(o_ref.dtype)


def _project_transposed(a, wt, out_dtype, tm=1024, tn=512, name="project_t"):
    m, k = a.shape
    n = wt.shape[0]
    tm, tn = min(tm, m), min(tn, n)
    return pl.pallas_call(
        _proj_t_kernel,
        grid=(m // tm, n // tn),
        in_specs=[pl.BlockSpec((tm, k), lambda i, j: (i, 0)), pl.BlockSpec((tn, k), lambda i, j: (j, 0))],
        out_specs=pl.BlockSpec((tn, tm), lambda i, j: (j, i)),
        out_shape=jax.ShapeDtypeStruct((n, m), out_dtype),
        compiler_params=_params("arbitrary", "arbitrary"),
        name=name,
    )(a, wt)


def _cumsum_kernel(x_ref, o_ref):
    x = x_ref[...]
    n = x.shape[1]
    lane = lax.broadcasted_iota(jnp.int32, x.shape, 1)
    k = 1
    while k < n:
        x = x + jnp.where(lane >= k, pltpu.roll(x, k, axis=1), 0.0)
        k *= 2
    g = -LOG2E * x
    hi = g.astype(BF16).astype(F32)
    mid = (g - hi).astype(BF16).astype(F32)
    o_ref[0] = hi
    o_ref[1] = mid
    o_ref[2] = (g - hi - mid).astype(BF16).astype(F32)


def _forget_bias_terms(x):
    bsz, r, n = x.shape
    return pl.pallas_call(
        _cumsum_kernel,
        grid=(bsz,),
        in_specs=[pl.BlockSpec((None, r, n), lambda b: (b, 0, 0))],
        out_specs=pl.BlockSpec((None, BIAS_TERMS, r, n), lambda b: (b, 0, 0, 0)),
        out_shape=jax.ShapeDtypeStruct((bsz, BIAS_TERMS, r, n), F32),
        compiler_params=_params("arbitrary"),
        name="forget_cumsum",
    )(x)


def _attn_kernel(n_heads, qi_ref, kj_ref, q_ref, k_ref, vt_ref, bias_ref, o_ref, m_scr, l_scr, acc_scr):
    step = pl.program_id(1)
    i = qi_ref[step]
    j = kj_ref[step]
    tq = q_ref.shape[0]
    tk = k_ref.shape[0]
    ratio = tq // tk

    @pl.when(j == 0)
    def _():
        m_scr[...] = jnp.full(m_scr.shape, MASK_VALUE, F32)
        l_scr[...] = jnp.zeros(l_scr.shape, F32)
        acc_scr[...] = jnp.zeros(acc_scr.shape, F32)

    def update(masked):
        bias = bias_ref[...]
        lane = lax.broadcasted_iota(jnp.int32, (tq, LANES), 1)
        if masked:
            key = j * tk + lax.broadcasted_iota(jnp.int32, (tk, tq), 0)
            qry = i * tq + lax.broadcasted_iota(jnp.int32, (tk, tq), 1)
            keep = key <= qry
        def scores(h):
            sl = slice(h * LANES, (h + 1) * LANES)
            pick = jnp.where((lane >= BIAS_TERMS * h) & (lane < BIAS_TERMS * (h + 1)), 1.0, 0.0).astype(BF16)
            qq = jnp.concatenate([q_ref[:, sl], pick], axis=1)
            kk = jnp.concatenate([k_ref[:, sl], bias], axis=1)
            return lax.dot_general(kk, qq, (((1,), (1,)), ((), ())), preferred_element_type=F32)

        st_next = scores(0)
        for h in range(n_heads):
            sl = slice(h * LANES, (h + 1) * LANES)
            st = st_next
            if h + 1 < n_heads:
                st_next = scores(h + 1)
            if masked:
                st = jnp.where(keep, st, MASK_VALUE)
            m_prev = m_scr[h]
            m_new = jnp.maximum(m_prev, jnp.max(st, axis=0, keepdims=True))
            alpha = jnp.exp2(m_prev - m_new)
            p = jnp.exp2(st - m_new)
            l_scr[h] = alpha * l_scr[h] + jnp.sum(p, axis=0, keepdims=True)
            acc_scr[h] = alpha * acc_scr[h] + jnp.dot(vt_ref[sl, :], p.astype(BF16), preferred_element_type=F32)
            m_scr[h] = m_new

    @pl.when(j < i * ratio)
    def _():
        update(False)

    @pl.when(j >= i * ratio)
    def _():
        update(True)

    @pl.when(j == (i + 1) * ratio - 1)
    def _():
        for h in range(n_heads):
            o_ref[:, h * LANES:(h + 1) * LANES] = (acc_scr[h] / l_scr[h]).T.astype(o_ref.dtype)


def _fox_attention(qk, vt, bias, n_heads, bsz, seq, tq=1024, tk=512):
    tq, tk = min(tq, seq), min(tk, seq)
    nq, nk = seq // tq, seq // tk
    ratio = tq // tk
    width = n_heads * LANES
    pairs = [(i, j) for i in range(nq) for j in range((i + 1) * ratio)]
    qi = jnp.asarray(np.array([p[0] for p in pairs], np.int32))
    kj = jnp.asarray(np.array([p[1] for p in pairs], np.int32))
    grid_spec = pltpu.PrefetchScalarGridSpec(
        num_scalar_prefetch=2,
        grid=(bsz, len(pairs)),
        in_specs=[
            pl.BlockSpec((tq, width), lambda b, s, qi, kj: (b * nq + qi[s], 0)),
            pl.BlockSpec((tk, width), lambda b, s, qi, kj: (b * nk + kj[s], 1)),
            pl.BlockSpec((width, tk), lambda b, s, qi, kj: (0, b * nk + kj[s])),
            pl.BlockSpec((tk, LANES), lambda b, s, qi, kj: (b * nk + kj[s], 0)),
        ],
        out_specs=pl.BlockSpec((tq, width), lambda b, s, qi, kj: (b * nq + qi[s], 0)),
        scratch_shapes=[pltpu.VMEM((n_heads, 1, tq), F32), pltpu.VMEM((n_heads, 1, tq), F32),
                        pltpu.VMEM((n_heads, LANES, tq), F32)],
    )
    return pl.pallas_call(
        functools.partial(_attn_kernel, n_heads),
        grid_spec=grid_spec,
        out_shape=jax.ShapeDtypeStruct((bsz * seq, width), BF16),
        compiler_params=_params("arbitrary", "arbitrary"),
        name="fox_attention",
    )(qi, kj, qk, qk, vt, bias)


def _ssm_prep_kernel(are_ref, aim_ref, ldt_ref, bre_ref, bim_ref, cre_ref, cim_ref, bbre_ref, bbim_ref):
    lam_re = are_ref[...]
    lam_im = aim_ref[...]
    dt = jnp.exp(ldt_ref[...])
    mag = jnp.exp(lam_re * dt)
    ab_re = mag * jnp.cos(lam_im * dt)
    ab_im = mag * jnp.sin(lam_im * dt)
    den = lam_re * lam_re + lam_im * lam_im
    nr = ab_re - 1.0
    coef_re = (nr * lam_re + ab_im * lam_im) / den
    coef_im = (ab_im * lam_re - nr * lam_im) / den
    b_re = bre_ref[...]
    b_im = bim_ref[...]
    bbre_ref[...] = coef_re * b_re - coef_im * b_im
    bbim_ref[...] = coef_re * b_im + coef_im * b_re
    pr, pi = ab_re, ab_im
    pows = []
    for _ in range(SUBLANES):
        pows.append((pr, pi))
        pr, pi = pr * ab_re - pi * ab_im, pr * ab_im + pi * ab_re
    row = lax.broadcasted_iota(jnp.int32, (SUBLANES, lam_re.shape[1]), 0)
    for idx, k in enumerate((1, 2, 4)):
        cre_ref[idx] = jnp.where(row >= k, pows[k - 1][0], 0.0)
        cim_ref[idx] = jnp.where(row >= k, pows[k - 1][1], 0.0)
    cre_ref[3] = jnp.concatenate([p[0] for p in pows], axis=0)
    cim_ref[3] = jnp.concatenate([p[1] for p in pows], axis=0)


def _ssm_prep(a_re, a_im, log_dt, b_re, b_im):
    g, p = a_re.shape
    gc = b_re.shape[2]
    gp = g * p
    flat = lambda a: a.reshape(1, gp)
    ldt = jnp.broadcast_to(log_dt[:, None], (g, p)).reshape(1, gp)
    bt = lambda b: b.transpose(2, 0, 1).reshape(gc, gp)
    return pl.pallas_call(
        _ssm_prep_kernel,
        out_shape=[jax.ShapeDtypeStruct((4, SUBLANES, gp), F32)] * 2 + [jax.ShapeDtypeStruct((gc, gp), F32)] * 2,
        name="ssm_discretize",
    )(flat(a_re), flat(a_im), ldt, bt(b_re), bt(b_im))


def _ssm_kernel(lane_chunk, u_ref, bre_ref, bim_ref, cfre_ref, cfim_ref, ctre_ref, ctim_ref, d_ref,
                wg_ref, bg_ref, o_ref, xre_scr, xim_scr, car_scr, cai_scr):
    tt = u_ref.shape[0]
    gp = xre_scr.shape[1]

    @pl.when(pl.program_id(1) == 0)
    def _():
        car_scr[...] = jnp.zeros(car_scr.shape, F32)
        cai_scr[...] = jnp.zeros(cai_scr.shape, F32)

    u = u_ref[...]
    ub = u.astype(BF16)
    xre_scr[...] = jnp.dot(ub, bre_ref[...], preferred_element_type=F32)
    xim_scr[...] = jnp.dot(ub, bim_ref[...], preferred_element_type=F32)

    for c0 in range(0, gp, lane_chunk):
        cols = slice(c0, c0 + lane_chunk)

        def body(r, carry, cols=cols):
            cr, ci = carry
            coefs = [(cfre_ref[n, :, cols], cfim_ref[n, :, cols]) for n in range(4)]
            rows = pl.ds(pl.multiple_of(r * SUBLANES, SUBLANES), SUBLANES)
            xr = xre_scr[rows, cols]
            xi = xim_scr[rows, cols]
            for n, k in enumerate((1, 2, 4)):
                ar, ai = coefs[n]
                sr = pltpu.roll(xr, k, axis=0)
                si = pltpu.roll(xi, k, axis=0)
                xr, xi = xr + ar * sr - ai * si, xi + ar * si + ai * sr
            pr, pi = coefs[3]
            xr, xi = xr + pr * cr - pi * ci, xi + pr * ci + pi * cr
            xre_scr[rows, cols] = xr
            xim_scr[rows, cols] = xi
            last = slice(SUBLANES - 1, SUBLANES)
            return (jnp.broadcast_to(xr[last, :], xr.shape), jnp.broadcast_to(xi[last, :], xi.shape))

        cr, ci = lax.fori_loop(0, tt // SUBLANES, body, (car_scr[:, cols], cai_scr[:, cols]))
        car_scr[:, cols] = cr
        cai_scr[:, cols] = ci

    y = (jnp.dot(xre_scr[...].astype(BF16), ctre_ref[...], preferred_element_type=F32)
         - jnp.dot(xim_scr[...].astype(BF16), ctim_ref[...], preferred_element_type=F32)
         + d_ref[...] * u)
    g = _gelu_exact(y)
    gate = jnp.dot(g.astype(BF16), wg_ref[...], preferred_element_type=F32) + bg_ref[...]
    o_ref[...] = (g * jax.nn.sigmoid(gate)).astype(o_ref.dtype)


def _block_diag_rows(m, groups):
    gc, gp = m.shape
    p = gp // groups
    rows = jnp.tile(m, (groups, 1))
    rg = lax.broadcasted_iota(jnp.int32, (groups * gc, gp), 0) // gc
    cg = lax.broadcasted_iota(jnp.int32, (groups * gc, gp), 1) // p
    return jnp.where(rg == cg, rows, 0.0)


def _s5_glu(u3, coef_re, coef_im, bb_re, bb_im, c_re, c_im, d_skip, w_glu, b_glu, tt=512, lane_chunk=512):
    bsz, seq, width = u3.shape
    g, gc, p = c_re.shape
    gp = g * p
    tt = min(tt, seq)
    lane_chunk = min(lane_chunk, gp)
    bd_b_re = _block_diag_rows(bb_re, g).astype(BF16)
    bd_b_im = _block_diag_rows(bb_im, g).astype(BF16)
    ct = lambda c: _block_diag_rows(c.transpose(1, 0, 2).reshape(gc, gp), g).T.astype(BF16)
    const = lambda shape: pl.BlockSpec(shape, lambda b, i: tuple(0 for _ in shape))
    return pl.pallas_call(
        functools.partial(_ssm_kernel, lane_chunk),
        grid=(bsz, seq // tt),
        in_specs=[pl.BlockSpec((None, tt, width), lambda b, i: (b, i, 0)),
                  const((width, gp)), const((width, gp)),
                  const((4, SUBLANES, gp)), const((4, SUBLANES, gp)),
                  const((gp, width)), const((gp, width)),
                  const((1, width)), const((width, width)), const((1, width))],
        out_specs=pl.BlockSpec((None, tt, width), lambda b, i: (b, i, 0)),
        out_shape=jax.ShapeDtypeStruct((bsz, seq, width), BF16),
        scratch_shapes=[pltpu.VMEM((tt, gp), F32), pltpu.VMEM((tt, gp), F32),
                        pltpu.VMEM((SUBLANES, gp), F32), pltpu.VMEM((SUBLANES, gp), F32)],
        compiler_params=_params("arbitrary", "arbitrary"),
        name="s5_glu",
    )(u3, bd_b_re, bd_b_im, coef_re, coef_im, ct(c_re), ct(c_im),
      d_skip.reshape(1, width).astype(F32), w_glu.astype(BF16), b_glu.reshape(1, width).astype(F32))


def _merge_kernel(ys_ref, ya_ref, gt_ref, wsu_ref, wau_ref, o_ref):
    d = o_ref.shape[1]
    ms = jnp.dot(ys_ref[...], wsu_ref[...], preferred_element_type=F32)
    ma = jnp.dot(ya_ref[...], wau_ref[...], preferred_element_type=F32)
    o_ref[...] = (gt_ref[:, :d].astype(F32) * ms + gt_ref[:, d:].astype(F32) * ma).astype(o_ref.dtype)


def _merge(ys, ya, gates, w_ssm_up, w_attn_up, tm=512):
    m = ys.shape[0]
    d = w_ssm_up.shape[1]
    tm = min(tm, m)
    row = lambda w: pl.BlockSpec((tm, w), lambda i: (i, 0))
    const = lambda a: pl.BlockSpec(a.shape, lambda i: (0, 0))
    return pl.pallas_call(
        _merge_kernel,
        grid=(m // tm,),
        in_specs=[row(ys.shape[1]), row(ya.shape[1]), row(gates.shape[1]), const(w_ssm_up), const(w_attn_up)],
        out_specs=row(d),
        out_shape=jax.ShapeDtypeStruct((m, d), BF16),
        compiler_params=_params("arbitrary"),
        name="gated_merge",
    )(ys, ya, gates, w_ssm_up, w_attn_up)


def _out_proj_kernel(mg_ref, w_ref, x_ref, g1_ref, nw_ref, sh_ref, sc_ref, x1_ref, h2_ref):
    x1 = x_ref[...] + g1_ref[...] * jnp.dot(mg_ref[...], w_ref[...], preferred_element_type=F32)
    x1_ref[...] = x1
    h2_ref[...] = _norm_mod(x1, nw_ref[...], sh_ref[...], sc_ref[...]).astype(h2_ref.dtype)


def _out_proj(merged3, w_out, x3, g1, norm_w, shift, scale, tm=512):
    bsz, seq, d = x3.shape
    tm = min(tm, seq)
    row = pl.BlockSpec((None, tm, d), lambda b, i: (b, i, 0))
    vec = pl.BlockSpec((None, 1, d), lambda b, i: (b, 0, 0))
    r3 = lambda a: a.reshape(bsz, 1, d)
    return pl.pallas_call(
        _out_proj_kernel,
        grid=(bsz, seq // tm),
        in_specs=[row, pl.BlockSpec((d, d), lambda b, i: (0, 0)), row, vec,
                  pl.BlockSpec((1, d), lambda b, i: (0, 0)), vec, vec],
        out_specs=[row, row],
        out_shape=[jax.ShapeDtypeStruct((bsz, seq, d), F32), jax.ShapeDtypeStruct((bsz, seq, d), BF16)],
        compiler_params=_params("arbitrary", "arbitrary"),
        name="out_proj_residual",
    )(merged3, w_out, x3, r3(g1), norm_w.reshape(1, d), r3(shift), r3(scale))


def _peer_scores_kernel(h_ref, wq_ref, keys_ref, o_ref):
    q = jnp.dot(h_ref[...], wq_ref[...], preferred_element_type=F32).astype(BF16)
    n_hp = keys_ref.shape[0]
    nc = o_ref.shape[1]
    for hp in range(n_hp):
        dk = keys_ref.shape[2]
        st = lax.dot_general(keys_ref[hp], q[:, hp * dk:(hp + 1) * dk], (((1,), (1,)), ((), ())),
                             preferred_element_type=F32)
        for c in range(nc):
            o_ref[hp, c] = st[:, c * LANES:(c + 1) * LANES]


def _peer_scores(h2, w_q, keys, tm=512):
    m, d = h2.shape
    n_hp, n_keys, dk = keys.shape
    tm = min(tm, m)
    nc = tm // LANES
    return pl.pallas_call(
        _peer_scores_kernel,
        grid=(m // tm,),
        in_specs=[pl.BlockSpec((tm, d), lambda i: (i, 0)),
                  pl.BlockSpec(w_q.shape, lambda i: (0, 0)),
                  pl.BlockSpec(keys.shape, lambda i: (0, 0, 0))],
        out_specs=pl.BlockSpec((n_hp, nc, n_keys, LANES), lambda i: (0, i, 0, 0)),
        out_shape=jax.ShapeDtypeStruct((n_hp, m // LANES, n_keys, LANES), F32),
        compiler_params=_params("arbitrary"),
        name="peer_scores",
    )(h2, w_q, keys)


def _candidate_pairs(k):
    return [(x, y) for x in range(k) for y in range(k) if (x + 1) * (y + 1) <= k]


def _top_values(s, k):
    vals, cnts = [], []
    for _ in range(k):
        m = jnp.max(s, axis=0, keepdims=True)
        eq = s == m
        cnts.append(jnp.sum(jnp.where(eq, 1.0, 0.0), axis=0, keepdims=True))
        vals.append(m)
        s = jnp.where(eq, -jnp.inf, s)
    return vals, cnts


def _peer_route(sc_ref, t1_scr, e1_scr, e2_scr, cand_scr, mult_scr, n_heads, nc):
    k = PEER_TOPK
    pairs = _candidate_pairs(k)
    n_rows = cand_scr.shape[0]

    def body(idx, carry):
        h = idx // nc
        c = idx % nc
        s1 = sc_ref[2 * h, c]
        s2 = sc_ref[2 * h + 1, c]
        a, ma = _top_values(s1, k)
        b, mb = _top_values(s2, k)
        cand_scr[...] = jnp.full(cand_scr.shape, -jnp.inf, F32)
        mult_scr[...] = jnp.zeros(mult_scr.shape, F32)
        for r, (x, y) in enumerate(pairs):
            cand_scr[r:r + 1, :] = a[x] + b[y]
            mult_scr[r:r + 1, :] = ma[x] * mb[y]
        cand = cand_scr[...]
        mult = mult_scr[...]
        work = cand
        remaining = jnp.full((1, LANES), float(k), F32)
        tau = jnp.full((1, LANES), -jnp.inf, F32)
        for _ in range(k):
            m = jnp.max(work, axis=0, keepdims=True)
            eq = work == m
            cnt = jnp.sum(jnp.where(eq, mult, 0.0), axis=0, keepdims=True)
            after = remaining - cnt
            tau = jnp.where((remaining > 0.0) & (after <= 0.0), m, tau)
            remaining = after
            work = jnp.where(eq, -jnp.inf, work)
        top = a[0] + b[0]
        z = jnp.sum(jnp.where(cand >= tau, jnp.exp(cand - top) * mult, 0.0), axis=0, keepdims=True)
        t1 = jnp.full(s1.shape, jnp.inf, F32)
        for y in range(k):
            t1 = jnp.where(s1 + b[y] >= tau, b[y], t1)
        t1_scr[h, c] = t1
        e1_scr[h, c] = jnp.exp(s1 - a[0]) / z
        e2_scr[h, c] = jnp.exp(s2 - b[0])
        return carry

    lax.fori_loop(0, n_heads * nc, body, 0)
    del n_rows


def _peer_dense_kernel(n_heads, h_ref, u_ref, v_ref, sc_ref, o_ref,
                       t1_scr, e1_scr, e2_scr, cand_scr, mult_scr, acc_scr, g_scr):
    e = pl.program_id(1)
    et, tm = g_scr.shape
    nc = tm // LANES
    n_keys = sc_ref.shape[2]

    @pl.when(e == 0)
    def _():
        _peer_route(sc_ref, t1_scr, e1_scr, e2_scr, cand_scr, mult_scr, n_heads, nc)
        acc_scr[...] = jnp.zeros(acc_scr.shape, F32)

    at = lax.dot_general(u_ref[...], h_ref[...], (((1,), (1,)), ((), ())), preferred_element_type=F32)
    for ib in range(et // n_keys):
        i = e * (et // n_keys) + ib
        for c in range(nc):
            w = jnp.zeros((n_keys, LANES), F32)
            for h in range(n_heads):
                t1 = t1_scr[h, c, pl.ds(i, 1), :]
                e1 = e1_scr[h, c, pl.ds(i, 1), :]
                w = w + jnp.where(sc_ref[2 * h + 1, c] >= t1, e2_scr[h, c], 0.0) * e1
            a_blk = at[ib * n_keys:(ib + 1) * n_keys, c * LANES:(c + 1) * LANES]
            g_scr[ib * n_keys:(ib + 1) * n_keys, c * LANES:(c + 1) * LANES] = (_gelu_exact(a_blk) * w).astype(BF16)
    acc_scr[...] += lax.dot_general(g_scr[...], v_ref[...], (((0,), (0,)), ((), ())), preferred_element_type=F32)

    @pl.when(e == pl.num_programs(1) - 1)
    def _():
        o_ref[...] = acc_scr[...]


def _peer_dense(h2, u_tab, v_tab, scores, n_heads, tm=512, et=512):
    m, d = h2.shape
    n_exp = u_tab.shape[0]
    n_hp, _, n_keys, _ = scores.shape
    tm = min(tm, m)
    nc = tm // LANES
    n_cand = -(-len(_candidate_pairs(PEER_TOPK)) // SUBLANES) * SUBLANES
    per_tok = pltpu.VMEM((n_heads, nc, n_keys, LANES), F32)
    return pl.pallas_call(
        functools.partial(_peer_dense_kernel, n_heads),
        grid=(m // tm, n_exp // et),
        in_specs=[pl.BlockSpec((tm, d), lambda i, e: (i, 0)),
                  pl.BlockSpec((et, d), lambda i, e: (e, 0)),
                  pl.BlockSpec((et, d), lambda i, e: (e, 0)),
                  pl.BlockSpec((n_hp, nc, n_keys, LANES), lambda i, e: (0, i, 0, 0))],
        out_specs=pl.BlockSpec((tm, d), lambda i, e: (i, 0)),
        out_shape=jax.ShapeDtypeStruct((m, d), F32),
        scratch_shapes=[per_tok, per_tok, per_tok,
                        pltpu.VMEM((n_cand, LANES), F32), pltpu.VMEM((n_cand, LANES), F32),
                        pltpu.VMEM((tm, d), F32), pltpu.VMEM((et, tm), BF16)],
        compiler_params=_params("arbitrary", "arbitrary"),
        name="peer_dense",
    )(h2, u_tab, v_tab, scores)


def _final_kernel(x_ref, g_ref, p_ref, o_ref):
    o_ref[...] = x_ref[...] + g_ref[...] * p_ref[...]


def _final_residual(x3, g2, peer3, tm=512):
    bsz, seq, d = x3.shape
    tm = min(tm, seq)
    row = pl.BlockSpec((None, tm, d), lambda b, i: (b, i, 0))
    return pl.pallas_call(
        _final_kernel,
        grid=(bsz, seq // tm),
        in_specs=[row, pl.BlockSpec((None, 1, d), lambda b, i: (b, 0, 0)), row],
        out_specs=row,
        out_shape=jax.ShapeDtypeStruct((bsz, seq, d), F32),
        compiler_params=_params("arbitrary", "arbitrary"),
        name="final_residual",
    )(x3, g2.reshape(bsz, 1, d), peer3)


def _layer(x, cond_in, l, w_ada, b_ada, norm1_w, w_in, b_forget, q_norm_w, k_norm_w,
           ssm_A_re, ssm_A_im, ssm_log_dt, ssm_B_re, ssm_B_im, ssm_C_re, ssm_C_im, ssm_D,
           w_glu, b_glu, w_ssm_up, w_attn_up, w_out, norm2_w, w_peer_q, peer_sub_keys, peer_u, peer_v):
    bsz, seq, d = x.shape
    tokens = bsz * seq
    n_heads = b_forget.shape[1]
    head_dim = q_norm_w.shape[1]
    assert head_dim == LANES
    attn_w = n_heads * head_dim
    ssm_w = w_glu.shape[1]
    peer_heads = peer_sub_keys.shape[1]

    mod = _adaln(cond_in, w_ada[l], b_ada[l])
    sh1, sc1, g1, sh2, sc2, g2 = jnp.split(mod, N_MOD, axis=-1)

    h = _norm_modulate(x, norm1_w[l], sh1, sc1).reshape(tokens, d)
    wi = w_in[l].astype(BF16)
    o_qk, o_v, o_f, o_u, o_g = 0, 2 * attn_w, 3 * attn_w, 3 * attn_w + n_heads, 3 * attn_w + n_heads + ssm_w
    qk_w = jnp.concatenate([jnp.tile(q_norm_w[l] * (head_dim ** -0.5 * LOG2E), n_heads),
                            jnp.tile(k_norm_w[l], n_heads)])
    qk = _project(h, wi[:, o_qk:o_v], BF16, _ep_head_rms, qk_w, name="proj_qk")
    vt = _project_transposed(h, wi[:, o_v:o_f].T, BF16, name="proj_v")
    u = _project(h, wi[:, o_u:o_g], F32, name="proj_u")
    gates = _project(h, wi[:, o_g:], BF16, _ep_sigmoid, name="proj_gates")
    w_f = jnp.zeros((d, LANES), BF16).at[:, :n_heads].set(wi[:, o_f:o_u])
    b_f = jnp.zeros((LANES,), F32).at[:n_heads].set(b_forget[l])
    log_f = _project(h, w_f, F32, _ep_log_sigmoid, b_f, name="proj_forget")
    log_f_t = log_f.reshape(bsz, seq, LANES)[:, :, :SUBLANES].transpose(0, 2, 1)
    terms = _forget_bias_terms(log_f_t)[:, :, :n_heads]
    bias = jnp.zeros((tokens, LANES), BF16).at[:, :BIAS_TERMS * n_heads].set(
        terms.transpose(0, 3, 2, 1).reshape(tokens, BIAS_TERMS * n_heads).astype(BF16))
    y_attn = _fox_attention(qk, vt, bias, n_heads, bsz, seq)

    coef_re, coef_im, bb_re, bb_im = _ssm_prep(ssm_A_re[l], ssm_A_im[l], ssm_log_dt[l], ssm_B_re[l], ssm_B_im[l])
    y_ssm = _s5_glu(u.reshape(bsz, seq, ssm_w), coef_re, coef_im, bb_re, bb_im, ssm_C_re[l], ssm_C_im[l],
                    ssm_D[l], w_glu[l], b_glu[l]).reshape(tokens, ssm_w)

    merged = _merge(y_ssm, y_attn, gates, w_ssm_up[l].astype(BF16), w_attn_up[l].astype(BF16))
    x1, h2 = _out_proj(merged.reshape(bsz, seq, d), w_out[l].astype(BF16), x, g1, norm2_w[l], sh2, sc2)

    keys = peer_sub_keys[l].reshape(2 * peer_heads, peer_sub_keys.shape[3], peer_sub_keys.shape[4]).astype(BF16)
    h2f = h2.reshape(tokens, d)
    scores = _peer_scores(h2f, w_peer_q[l].astype(BF16), keys)
    peer = _peer_dense(h2f, peer_u[l].astype(BF16), peer_v[l].astype(BF16), scores, peer_heads)
    return _final_residual(x1, g2, peer.reshape(bsz, seq, d))


def kernel(x, c, w_ada, b_ada, norm1_w, w_in, b_forget, q_norm_w, k_norm_w, ssm_A_re, ssm_A_im, ssm_log_dt,
           ssm_B_re, ssm_B_im, ssm_C_re, ssm_C_im, ssm_D, w_glu, b_glu, w_ssm_up, w_attn_up, w_out, norm2_w,
           w_peer_q, peer_sub_keys, peer_u, peer_v):
    for l in range(w_ada.shape[0]):
        x = _layer(x, c, l, w_ada, b_ada, norm1_w, w_in, b_forget, q_norm_w, k_norm_w,
                   ssm_A_re, ssm_A_im, ssm_log_dt, ssm_B_re, ssm_B_im, ssm_C_re, ssm_C_im, ssm_D,
                   w_glu, b_glu, w_ssm_up, w_attn_up, w_out, norm2_w, w_peer_q, peer_sub_keys, peer_u, peer_v)
    return x
```

```python
import functools
import math

import numpy as np
import jax
import jax.numpy as jnp
from jax import lax
from jax.experimental import pallas as pl
from jax.experimental.pallas import tpu as pltpu

F32 = jnp.float32
BF16 = jnp.bfloat16

RMS_EPS = 1e-6
MASK_VALUE = -1e30
N_MOD = 6
PEER_TOPK = 16
SQRT_HALF = math.sqrt(0.5)
LOG2E = math.log2(math.e)
BIAS_TERMS = 3

LANES = 128
SUBLANES = 8
VMEM_LIMIT_BYTES = 56 * 1024 * 1024


def _params(*sem):
    return pltpu.CompilerParams(dimension_semantics=sem, vmem_limit_bytes=VMEM_LIMIT_BYTES)


def _gelu_exact(x):
    return 0.5 * x * (1.0 + lax.erf(x * SQRT_HALF))


def _adaln_kernel(c_ref, w_ref, b_ref, o_ref):
    c = c_ref[...]
    cond = c * jax.nn.sigmoid(c)
    o_ref[...] = jnp.dot(cond.astype(BF16), w_ref[...].astype(BF16), preferred_element_type=F32) + b_ref[...]


def _adaln(c, w, b, tn=512):
    bsz, d = c.shape
    n = w.shape[1]
    c8 = jnp.zeros((SUBLANES, d), F32).at[:bsz].set(c)
    out = pl.pallas_call(
        _adaln_kernel,
        grid=(n // tn,),
        in_specs=[pl.BlockSpec((SUBLANES, d), lambda j: (0, 0)),
                  pl.BlockSpec((d, tn), lambda j: (0, j)),
                  pl.BlockSpec((1, tn), lambda j: (0, j))],
        out_specs=pl.BlockSpec((SUBLANES, tn), lambda j: (0, j)),
        out_shape=jax.ShapeDtypeStruct((SUBLANES, n), F32),
        compiler_params=_params("arbitrary"),
        name="adaln",
    )(c8, w, b.reshape(1, n))
    return out[:bsz]


def _norm_mod(x, w, shift, scale):
    xf = x * lax.rsqrt(jnp.mean(x * x, axis=-1, keepdims=True) + RMS_EPS)
    return (xf * w) * (1.0 + scale) + shift


def _norm_mod_kernel(x_ref, w_ref, sh_ref, sc_ref, o_ref):
    o_ref[...] = _norm_mod(x_ref[...], w_ref[...], sh_ref[...], sc_ref[...]).astype(o_ref.dtype)


def _norm_modulate(x3, w, shift, scale, tm=512):
    bsz, seq, d = x3.shape
    tm = min(tm, seq)
    vec = pl.BlockSpec((None, 1, d), lambda b, i: (b, 0, 0))
    return pl.pallas_call(
        _norm_mod_kernel,
        grid=(bsz, seq // tm),
        in_specs=[pl.BlockSpec((None, tm, d), lambda b, i: (b, i, 0)),
                  pl.BlockSpec((1, d), lambda b, i: (0, 0)), vec, vec],
        out_specs=pl.BlockSpec((None, tm, d), lambda b, i: (b, i, 0)),
        out_shape=jax.ShapeDtypeStruct((bsz, seq, d), BF16),
        compiler_params=_params("arbitrary", "arbitrary"),
        name="norm_modulate",
    )(x3, w.reshape(1, d), shift.reshape(bsz, 1, d), scale.reshape(bsz, 1, d))


def _ep_plain(acc):
    return acc


def _ep_sigmoid(acc):
    return jax.nn.sigmoid(acc)


def _ep_head_rms(acc, w):
    outs = []
    for s in range(0, acc.shape[1], LANES):
        t = acc[:, s:s + LANES]
        t = t * lax.rsqrt(jnp.mean(t * t, axis=-1, keepdims=True) + RMS_EPS)
        outs.append(t * w[:, s:s + LANES])
    return jnp.concatenate(outs, axis=1)


def _ep_log_sigmoid(acc, b):
    z = acc + b
    return jnp.minimum(z, 0.0) - jnp.log1p(jnp.exp(-jnp.abs(z)))


def _proj_kernel(epilogue, a_ref, w_ref, *rest):
    o_ref = rest[-1]
    acc = jnp.dot(a_ref[...], w_ref[...], preferred_element_type=F32)
    o_ref[...] = epilogue(acc, *[r[...] for r in rest[:-1]]).astype(o_ref.dtype)


def _project(a, w, out_dtype, epilogue=_ep_plain, row_vec=None, tm=1024, tn=512, name="project"):
    m, k = a.shape
    n = w.shape[1]
    tm, tn = min(tm, m), min(tn, n)
    in_specs = [pl.BlockSpec((tm, k), lambda i, j: (i, 0)), pl.BlockSpec((k, tn), lambda i, j: (0, j))]
    args = [a, w]
    if row_vec is not None:
        in_specs.append(pl.BlockSpec((1, tn), lambda i, j: (0, j)))
        args.append(row_vec.reshape(1, n).astype(F32))
    return pl.pallas_call(
        functools.partial(_proj_kernel, epilogue),
        grid=(m // tm, n // tn),
        in_specs=in_specs,
        out_specs=pl.BlockSpec((tm, tn), lambda i, j: (i, j)),
        out_shape=jax.ShapeDtypeStruct((m, n), out_dtype),
        compiler_params=_params("arbitrary", "arbitrary"),
        name=name,
    )(*args)


def _proj_t_kernel(a_ref, wt_ref, o_ref):
    o_ref[...] = lax.dot_general(wt_ref[...], a_ref[...], (((1,), (1,)), ((), ())),
                                 preferred_element_type=F32).astype(o_ref.dtype)


def _project_transposed(a, wt, out_dtype, tm=1024, tn=512, name="project_t"):
    m, k = a.shape
    n = wt.shape[0]
    tm, tn = min(tm, m), min(tn, n)
    return pl.pallas_call(
        _proj_t_kernel,
        grid=(m // tm, n // tn),
        in_specs=[pl.BlockSpec((tm, k), lambda i, j: (i, 0)), pl.BlockSpec((tn, k), lambda i, j: (j, 0))],
        out_specs=pl.BlockSpec((tn, tm), lambda i, j: (j, i)),
        out_shape=jax.ShapeDtypeStruct((n, m), out_dtype),
        compiler_params=_params("arbitrary", "arbitrary"),
        name=name,
    )(a, wt)


def _cumsum_kernel(x_ref, o_ref):
    x = x_ref[...]
    n = x.shape[1]
    lane = lax.broadcasted_iota(jnp.int32, x.shape, 1)
    k = 1
    while k < n:
        x = x + jnp.where(lane >= k, pltpu.roll(x, k, axis=1), 0.0)
        k *= 2
    g = -LOG2E * x
    hi = g.astype(BF16).astype(F32)
    mid = (g - hi).astype(BF16).astype(F32)
    o_ref[0] = hi
    o_ref[1] = mid
    o_ref[2] = (g - hi - mid).astype(BF16).astype(F32)


def _forget_bias_terms(x):
    bsz, r, n = x.shape
    return pl.pallas_call(
        _cumsum_kernel,
        grid=(bsz,),
        in_specs=[pl.BlockSpec((None, r, n), lambda b: (b, 0, 0))],
        out_specs=pl.BlockSpec((None, BIAS_TERMS, r, n), lambda b: (b, 0, 0, 0)),
        out_shape=jax.ShapeDtypeStruct((bsz, BIAS_TERMS, r, n), F32),
        compiler_params=_params("arbitrary"),
        name="forget_cumsum",
    )(x)


def _attn_kernel(n_heads, qi_ref, kj_ref, q_ref, k_ref, vt_ref, bias_ref, o_ref, m_scr, l_scr, acc_scr):
    step = pl.program_id(1)
    i = qi_ref[step]
    j = kj_ref[step]
    tq = q_ref.shape[0]
    tk = k_ref.shape[0]
    ratio = tq // tk

    @pl.when(j == 0)
    def _():
        m_scr[...] = jnp.full(m_scr.shape, MASK_VALUE, F32)
        l_scr[...] = jnp.zeros(l_scr.shape, F32)
        acc_scr[...] = jnp.zeros(acc_scr.shape, F32)

    def update(masked):
        bias = bias_ref[...]
        lane = lax.broadcasted_iota(jnp.int32, (tq, LANES), 1)
        if masked:
            key = j * tk + lax.broadcasted_iota(jnp.int32, (tk, tq), 0)
            qry = i * tq + lax.broadcasted_iota(jnp.int32, (tk, tq), 1)
            keep = key <= qry
        def scores(h):
            sl = slice(h * LANES, (h + 1) * LANES)
            pick = jnp.where((lane >= BIAS_TERMS * h) & (lane < BIAS_TERMS * (h + 1)), 1.0, 0.0).astype(BF16)
            qq = jnp.concatenate([q_ref[:, sl], pick], axis=1)
            kk = jnp.concatenate([k_ref[:, sl], bias], axis=1)
            return lax.dot_general(kk, qq, (((1,), (1,)), ((), ())), preferred_element_type=F32)

        st_next = scores(0)
        for h in range(n_heads):
            sl = slice(h * LANES, (h + 1) * LANES)
            st = st_next
            if h + 1 < n_heads:
                st_next = scores(h + 1)
            if masked:
                st = jnp.where(keep, st, MASK_VALUE)
            m_prev = m_scr[h]
            m_new = jnp.maximum(m_prev, jnp.max(st, axis=0, keepdims=True))
            alpha = jnp.exp2(m_prev - m_new)
            p = jnp.exp2(st - m_new)
            l_scr[h] = alpha * l_scr[h] + jnp.sum(p, axis=0, keepdims=True)
            acc_scr[h] = alpha * acc_scr[h] + jnp.dot(vt_ref[sl, :], p.astype(BF16), preferred_element_type=F32)
            m_scr[h] = m_new

    @pl.when(j < i * ratio)
    def _():
        update(False)

    @pl.when(j >= i * ratio)
    def _():
        update(True)

    @pl.when(j == (i + 1) * ratio - 1)
    def _():
        for h in range(n_heads):
            o_ref[:, h * LANES:(h + 1) * LANES] = (acc_scr[h] / l_scr[h]).T.astype(o_ref.dtype)


def _fox_attention(qk, vt, bias, n_heads, bsz, seq, tq=1024, tk=512):
    tq, tk = min(tq, seq), min(tk, seq)
    nq, nk = seq // tq, seq // tk
    ratio = tq // tk
    width = n_heads * LANES
    pairs = [(i, j) for i in range(nq) for j in range((i + 1) * ratio)]
    qi = jnp.asarray(np.array([p[0] for p in pairs], np.int32))
    kj = jnp.asarray(np.array([p[1] for p in pairs], np.int32))
    grid_spec = pltpu.PrefetchScalarGridSpec(
        num_scalar_prefetch=2,
        grid=(bsz, len(pairs)),
        in_specs=[
            pl.BlockSpec((tq, width), lambda b, s, qi, kj: (b * nq + qi[s], 0)),
            pl.BlockSpec((tk, width), lambda b, s, qi, kj: (b * nk + kj[s], 1)),
            pl.BlockSpec((width, tk), lambda b, s, qi, kj: (0, b * nk + kj[s])),
            pl.BlockSpec((tk, LANES), lambda b, s, qi, kj: (b * nk + kj[s], 0)),
        ],
        out_specs=pl.BlockSpec((tq, width), lambda b, s, qi, kj: (b * nq + qi[s], 0)),
        scratch_shapes=[pltpu.VMEM((n_heads, 1, tq), F32), pltpu.VMEM((n_heads, 1, tq), F32),
                        pltpu.VMEM((n_heads, LANES, tq), F32)],
    )
    return pl.pallas_call(
        functools.partial(_attn_kernel, n_heads),
        grid_spec=grid_spec,
        out_shape=jax.ShapeDtypeStruct((bsz * seq, width), BF16),
        compiler_params=_params("arbitrary", "arbitrary"),
        name="fox_attention",
    )(qi, kj, qk, qk, vt, bias)


def _ssm_prep_kernel(are_ref, aim_ref, ldt_ref, bre_ref, bim_ref, cre_ref, cim_ref, bbre_ref, bbim_ref):
    lam_re = are_ref[...]
    lam_im = aim_ref[...]
    dt = jnp.exp(ldt_ref[...])
    mag = jnp.exp(lam_re * dt)
    ab_re = mag * jnp.cos(lam_im * dt)
    ab_im = mag * jnp.sin(lam_im * dt)
    den = lam_re * lam_re + lam_im * lam_im
    nr = ab_re - 1.0
    coef_re = (nr * lam_re + ab_im * lam_im) / den
    coef_im = (ab_im * lam_re - nr * lam_im) / den
    b_re = bre_ref[...]
    b_im = bim_ref[...]
    bbre_ref[...] = coef_re * b_re - coef_im * b_im
    bbim_ref[...] = coef_re * b_im + coef_im * b_re
    pr, pi = ab_re, ab_im
    pows = []
    for _ in range(SUBLANES):
        pows.append((pr, pi))
        pr, pi = pr * ab_re - pi * ab_im, pr * ab_im + pi * ab_re
    row = lax.broadcasted_iota(jnp.int32, (SUBLANES, lam_re.shape[1]), 0)
    for idx, k in enumerate((1, 2, 4)):
        cre_ref[idx] = jnp.where(row >= k, pows[k - 1][0], 0.0)
        cim_ref[idx] = jnp.where(row >= k, pows[k - 1][1], 0.0)
    cre_ref[3] = jnp.concatenate([p[0] for p in pows], axis=0)
    cim_ref[3] = jnp.concatenate([p[1] for p in pows], axis=0)


def _ssm_prep(a_re, a_im, log_dt, b_re, b_im):
    g, p = a_re.shape
    gc = b_re.shape[2]
    gp = g * p
    flat = lambda a: a.reshape(1, gp)
    ldt = jnp.broadcast_to(log_dt[:, None], (g, p)).reshape(1, gp)
    bt = lambda b: b.transpose(2, 0, 1).reshape(gc, gp)
    return pl.pallas_call(
        _ssm_prep_kernel,
        out_shape=[jax.ShapeDtypeStruct((4, SUBLANES, gp), F32)] * 2 + [jax.ShapeDtypeStruct((gc, gp), F32)] * 2,
        name="ssm_discretize",
    )(flat(a_re), flat(a_im), ldt, bt(b_re), bt(b_im))


def _ssm_kernel(lane_chunk, u_ref, bre_ref, bim_ref, cfre_ref, cfim_ref, ctre_ref, ctim_ref, d_ref,
                wg_ref, bg_ref, o_ref, xre_scr, xim_scr, car_scr, cai_scr):
    tt = u_ref.shape[0]
    gp = xre_scr.shape[1]

    @pl.when(pl.program_id(1) == 0)
    def _():
        car_scr[...] = jnp.zeros(car_scr.shape, F32)
        cai_scr[...] = jnp.zeros(cai_scr.shape, F32)

    u = u_ref[...]
    ub = u.astype(BF16)
    xre_scr[...] = jnp.dot(ub, bre_ref[...], preferred_element_type=F32)
    xim_scr[...] = jnp.dot(ub, bim_ref[...], preferred_element_type=F32)

    for c0 in range(0, gp, lane_chunk):
        cols = slice(c0, c0 + lane_chunk)

        def body(r, carry, cols=cols):
            cr, ci = carry
            coefs = [(cfre_ref[n, :, cols], cfim_ref[n, :, cols]) for n in range(4)]
            rows = pl.ds(pl.multiple_of(r * SUBLANES, SUBLANES), SUBLANES)
            xr = xre_scr[rows, cols]
            xi = xim_scr[rows, cols]
            for n, k in enumerate((1, 2, 4)):
                ar, ai = coefs[n]
                sr = pltpu.roll(xr, k, axis=0)
                si = pltpu.roll(xi, k, axis=0)
                xr, xi = xr + ar * sr - ai * si, xi + ar * si + ai * sr
            pr, pi = coefs[3]
            xr, xi = xr + pr * cr - pi * ci, xi + pr * ci + pi * cr
            xre_scr[rows, cols] = xr
            xim_scr[rows, cols] = xi
            last = slice(SUBLANES - 1, SUBLANES)
            return (jnp.broadcast_to(xr[last, :], xr.shape), jnp.broadcast_to(xi[last, :], xi.shape))

        cr, ci = lax.fori_loop(0, tt // SUBLANES, body, (car_scr[:, cols], cai_scr[:, cols]))
        car_scr[:, cols] = cr
        cai_scr[:, cols] = ci

    y = (jnp.dot(xre_scr[...].astype(BF16), ctre_ref[...], preferred_element_type=F32)
         - jnp.dot(xim_scr[...].astype(BF16), ctim_ref[...], preferred_element_type=F32)
         + d_ref[...] * u)
    g = _gelu_exact(y)
    gate = jnp.dot(g.astype(BF16), wg_ref[...], preferred_element_type=F32) + bg_ref[...]
    o_ref[...] = (g * jax.nn.sigmoid(gate)).astype(o_ref.dtype)


def _block_diag_rows(m, groups):
    gc, gp = m.shape
    p = gp // groups
    rows = jnp.tile(m, (groups, 1))
    rg = lax.broadcasted_iota(jnp.int32, (groups * gc, gp), 0) // gc
    cg = lax.broadcasted_iota(jnp.int32, (groups * gc, gp), 1) // p
    return jnp.where(rg == cg, rows, 0.0)


def _s5_glu(u3, coef_re, coef_im, bb_re, bb_im, c_re, c_im, d_skip, w_glu, b_glu, tt=512, lane_chunk=512):
    bsz, seq, width = u3.shape
    g, gc, p = c_re.shape
    gp = g * p
    tt = min(tt, seq)
    lane_chunk = min(lane_chunk, gp)
    bd_b_re = _block_diag_rows(bb_re, g).astype(BF16)
    bd_b_im = _block_diag_rows(bb_im, g).astype(BF16)
    ct = lambda c: _block_diag_rows(c.transpose(1, 0, 2).reshape(gc, gp), g).T.astype(BF16)
    const = lambda shape: pl.BlockSpec(shape, lambda b, i: tuple(0 for _ in shape))
    return pl.pallas_call(
        functools.partial(_ssm_kernel, lane_chunk),
        grid=(bsz, seq // tt),
        in_specs=[pl.BlockSpec((None, tt, width), lambda b, i: (b, i, 0)),
                  const((width, gp)), const((width, gp)),
                  const((4, SUBLANES, gp)), const((4, SUBLANES, gp)),
                  const((gp, width)), const((gp, width)),
                  const((1, width)), const((width, width)), const((1, width))],
        out_specs=pl.BlockSpec((None, tt, width), lambda b, i: (b, i, 0)),
        out_shape=jax.ShapeDtypeStruct((bsz, seq, width), BF16),
        scratch_shapes=[pltpu.VMEM((tt, gp), F32), pltpu.VMEM((tt, gp), F32),
                        pltpu.VMEM((SUBLANES, gp), F32), pltpu.VMEM((SUBLANES, gp), F32)],
        compiler_params=_params("arbitrary", "arbitrary"),
        name="s5_glu",
    )(u3, bd_b_re, bd_b_im, coef_re, coef_im, ct(c_re), ct(c_im),
      d_skip.reshape(1, width).astype(F32), w_glu.astype(BF16), b_glu.reshape(1, width).astype(F32))


def _merge_kernel(ys_ref, ya_ref, gt_ref, wsu_ref, wau_ref, o_ref):
    d = o_ref.shape[1]
    ms = jnp.dot(ys_ref[...], wsu_ref[...], preferred_element_type=F32)
    ma = jnp.dot(ya_ref[...], wau_ref[...], preferred_element_type=F32)
    o_ref[...] = (gt_ref[:, :d].astype(F32) * ms + gt_ref[:, d:].astype(F32) * ma).astype(o_ref.dtype)


def _merge(ys, ya, gates, w_ssm_up, w_attn_up, tm=512):
    m = ys.shape[0]
    d = w_ssm_up.shape[1]
    tm = min(tm, m)
    row = lambda w: pl.BlockSpec((tm, w), lambda i: (i, 0))
    const = lambda a: pl.BlockSpec(a.shape, lambda i: (0, 0))
    return pl.pallas_call(
        _merge_kernel,
        grid=(m // tm,),
        in_specs=[row(ys.shape[1]), row(ya.shape[1]), row(gates.shape[1]), const(w_ssm_up), const(w_attn_up)],
        out_specs=row(d),
        out_shape=jax.ShapeDtypeStruct((m, d), BF16),
        compiler_params=_params("arbitrary"),
        name="gated_merge",
    )(ys, ya, gates, w_ssm_up, w_attn_up)


def _out_proj_kernel(mg_ref, w_ref, x_ref, g1_ref, nw_ref, sh_ref, sc_ref, x1_ref, h2_ref, h2t_ref):
    x1 = x_ref[...] + g1_ref[...] * jnp.dot(mg_ref[...], w_ref[...], preferred_element_type=F32)
    x1_ref[...] = x1
    h2 = _norm_mod(x1, nw_ref[...], sh_ref[...], sc_ref[...])
    h2_ref[...] = h2.astype(h2_ref.dtype)
    h2t_ref[...] = h2.T.astype(h2t_ref.dtype)


def _out_proj(merged3, w_out, x3, g1, norm_w, shift, scale, tm=512):
    bsz, seq, d = x3.shape
    tm = min(tm, seq)
    nt = seq // tm
    row = pl.BlockSpec((None, tm, d), lambda b, i: (b, i, 0))
    vec = pl.BlockSpec((None, 1, d), lambda b, i: (b, 0, 0))
    r3 = lambda a: a.reshape(bsz, 1, d)
    return pl.pallas_call(
        _out_proj_kernel,
        grid=(bsz, nt),
        in_specs=[row, pl.BlockSpec((d, d), lambda b, i: (0, 0)), row, vec,
                  pl.BlockSpec((1, d), lambda b, i: (0, 0)), vec, vec],
        out_specs=[row, row, pl.BlockSpec((d, tm), lambda b, i: (0, b * nt + i))],
        out_shape=[jax.ShapeDtypeStruct((bsz, seq, d), F32), jax.ShapeDtypeStruct((bsz, seq, d), BF16),
                   jax.ShapeDtypeStruct((d, bsz * seq), BF16)],
        compiler_params=_params("arbitrary", "arbitrary"),
        name="out_proj_residual",
    )(merged3, w_out, x3, r3(g1), norm_w.reshape(1, d), r3(shift), r3(scale))


def _peer_scores_kernel(h_ref, wq_ref, keys_ref, o_ref):
    q = jnp.dot(h_ref[...], wq_ref[...], preferred_element_type=F32).astype(BF16)
    n_hp = keys_ref.shape[0]
    nc = o_ref.shape[1]
    for hp in range(n_hp):
        dk = keys_ref.shape[2]
        st = lax.dot_general(keys_ref[hp], q[:, hp * dk:(hp + 1) * dk], (((1,), (1,)), ((), ())),
                             preferred_element_type=F32)
        for c in range(nc):
            o_ref[hp, c] = st[:, c * LANES:(c + 1) * LANES]


def _peer_scores(h2, w_q, keys, tm=512):
    m, d = h2.shape
    n_hp, n_keys, dk = keys.shape
    tm = min(tm, m)
    nc = tm // LANES
    return pl.pallas_call(
        _peer_scores_kernel,
        grid=(m // tm,),
        in_specs=[pl.BlockSpec((tm, d), lambda i: (i, 0)),
                  pl.BlockSpec(w_q.shape, lambda i: (0, 0)),
                  pl.BlockSpec(keys.shape, lambda i: (0, 0, 0))],
        out_specs=pl.BlockSpec((n_hp, nc, n_keys, LANES), lambda i: (0, i, 0, 0)),
        out_shape=jax.ShapeDtypeStruct((n_hp, m // LANES, n_keys, LANES), F32),
        compiler_params=_params("arbitrary"),
        name="peer_scores",
    )(h2, w_q, keys)


def _candidate_pairs(k):
    return [(x, y) for x in range(k) for y in range(k) if (x + 1) * (y + 1) <= k]


def _top_values(s, k):
    vals, cnts = [], []
    for _ in range(k):
        m = jnp.max(s, axis=0, keepdims=True)
        eq = s == m
        cnts.append(jnp.sum(jnp.where(eq, 1.0, 0.0), axis=0, keepdims=True))
        vals.append(m)
        s = jnp.where(eq, -jnp.inf, s)
    return vals, cnts


def _peer_route(sc_ref, t1_scr, e1_scr, e2_scr, cand_scr, mult_scr, n_heads, nc):
    k = PEER_TOPK
    pairs = _candidate_pairs(k)
    n_rows = cand_scr.shape[0]

    def body(idx, carry):
        h = idx // nc
        c = idx % nc
        s1 = sc_ref[2 * h, c]
        s2 = sc_ref[2 * h + 1, c]
        a, ma = _top_values(s1, k)
        b, mb = _top_values(s2, k)
        cand_scr[...] = jnp.full(cand_scr.shape, -jnp.inf, F32)
        mult_scr[...] = jnp.zeros(mult_scr.shape, F32)
        for r, (x, y) in enumerate(pairs):
            cand_scr[r:r + 1, :] = a[x] + b[y]
            mult_scr[r:r + 1, :] = ma[x] * mb[y]
        cand = cand_scr[...]
        mult = mult_scr[...]
        work = cand
        remaining = jnp.full((1, LANES), float(k), F32)
        tau = jnp.full((1, LANES), -jnp.inf, F32)
        for _ in range(k):
            m = jnp.max(work, axis=0, keepdims=True)
            eq = work == m
            cnt = jnp.sum(jnp.where(eq, mult, 0.0), axis=0, keepdims=True)
            after = remaining - cnt
            tau = jnp.where((remaining > 0.0) & (after <= 0.0), m, tau)
            remaining = after
            work = jnp.where(eq, -jnp.inf, work)
        top = a[0] + b[0]
        z = jnp.sum(jnp.where(cand >= tau, jnp.exp(cand - top) * mult, 0.0), axis=0, keepdims=True)
        t1 = jnp.full(s1.shape, jnp.inf, F32)
        for y in range(k):
            t1 = jnp.where(s1 + b[y] >= tau, b[y], t1)
        t1_scr[h, c] = t1
        e1_scr[h, c] = jnp.exp(s1 - a[0]) / z
        e2_scr[h, c] = jnp.exp(s2 - b[0])
        return carry

    lax.fori_loop(0, n_heads * nc, body, 0)
    del n_rows


def _peer_dense_kernel(n_heads, nw, ht_ref, u_ref, vt_ref, sc_ref, ot_ref,
                       t1_scr, e1_scr, e2_scr, cand_scr, mult_scr, g_scr):
    e = pl.program_id(1)
    et = u_ref.shape[0]
    tm = ht_ref.shape[1]
    nc = tm // LANES
    n_keys = sc_ref.shape[2]

    @pl.when(e == 0)
    def _():
        _peer_route(sc_ref, t1_scr, e1_scr, e2_scr, cand_scr, mult_scr, n_heads, nc)
        ot_ref[...] = jnp.zeros(ot_ref.shape, F32)

    def activations_t(n):
        return jnp.dot(u_ref[...], ht_ref[:, n * nw:(n + 1) * nw], preferred_element_type=F32)

    def gate(n, at):
        for cc in range(nw // LANES):
            c = n * (nw // LANES) + cc
            for ib in range(et // n_keys):
                i = e * (et // n_keys) + ib
                w = jnp.zeros((n_keys, LANES), F32)
                for h in range(n_heads):
                    t1 = t1_scr[h, c, pl.ds(i, 1), :]
                    e1 = e1_scr[h, c, pl.ds(i, 1), :]
                    w = w + jnp.where(sc_ref[2 * h + 1, c] >= t1, e2_scr[h, c], 0.0) * e1
                a_blk = at[ib * n_keys:(ib + 1) * n_keys, cc * LANES:(cc + 1) * LANES]
                g_scr[ib * n_keys:(ib + 1) * n_keys, c * LANES:(c + 1) * LANES] = (
                    _gelu_exact(a_blk) * w).astype(BF16)

    def accumulate(n):
        cols = slice(n * nw, (n + 1) * nw)
        ot_ref[:, cols] += jnp.dot(vt_ref[...], g_scr[:, cols], preferred_element_type=F32)

    n_groups = tm // nw
    at_next = activations_t(0)
    for n in range(n_groups):
        at = at_next
        if n + 1 < n_groups:
            at_next = activations_t(n + 1)
        gate(n, at)
        if n >= 1:
            accumulate(n - 1)
    accumulate(n_groups - 1)


def _peer_dense(h2t, u_tab, vt_tab, scores, n_heads, tm=512, et=1024, nw=256):
    d, m = h2t.shape
    n_exp = u_tab.shape[0]
    n_hp, _, n_keys, _ = scores.shape
    tm = min(tm, m)
    nw = min(nw, tm)
    nc = tm // LANES
    n_cand = -(-len(_candidate_pairs(PEER_TOPK)) // SUBLANES) * SUBLANES
    per_tok = pltpu.VMEM((n_heads, nc, n_keys, LANES), F32)
    return pl.pallas_call(
        functools.partial(_peer_dense_kernel, n_heads, nw),
        grid=(m // tm, n_exp // et),
        in_specs=[pl.BlockSpec((d, tm), lambda i, e: (0, i)),
                  pl.BlockSpec((et, d), lambda i, e: (e, 0)),
                  pl.BlockSpec((d, et), lambda i, e: (0, e)),
                  pl.BlockSpec((n_hp, nc, n_keys, LANES), lambda i, e: (0, i, 0, 0))],
        out_specs=pl.BlockSpec((d, tm), lambda i, e: (0, i)),
        out_shape=jax.ShapeDtypeStruct((d, m), F32),
        scratch_shapes=[per_tok, per_tok, per_tok,
                        pltpu.VMEM((n_cand, LANES), F32), pltpu.VMEM((n_cand, LANES), F32),
                        pltpu.VMEM((et, tm), BF16)],
        compiler_params=_params("arbitrary", "arbitrary"),
        name="peer_dense",
    )(h2t, u_tab, vt_tab, scores)


def _final_kernel(x_ref, g_ref, pt_ref, o_ref):
    o_ref[...] = x_ref[...] + g_ref[...] * pt_ref[...].T


def _final_residual(x3, g2, peer_t, tm=512):
    bsz, seq, d = x3.shape
    tm = min(tm, seq)
    nt = seq // tm
    row = pl.BlockSpec((None, tm, d), lambda b, i: (b, i, 0))
    return pl.pallas_call(
        _final_kernel,
        grid=(bsz, nt),
        in_specs=[row, pl.BlockSpec((None, 1, d), lambda b, i: (b, 0, 0)),
                  pl.BlockSpec((d, tm), lambda b, i: (0, b * nt + i))],
        out_specs=row,
        out_shape=jax.ShapeDtypeStruct((bsz, seq, d), F32),
        compiler_params=_params("arbitrary", "arbitrary"),
        name="final_residual",
    )(x3, g2.reshape(bsz, 1, d), peer_t)


def _layer(x, cond_in, l, w_ada, b_ada, norm1_w, w_in, b_forget, q_norm_w, k_norm_w,
           ssm_A_re, ssm_A_im, ssm_log_dt, ssm_B_re, ssm_B_im, ssm_C_re, ssm_C_im, ssm_D,
           w_glu, b_glu, w_ssm_up, w_attn_up, w_out, norm2_w, w_peer_q, peer_sub_keys, peer_u, peer_v):
    bsz, seq, d = x.shape
    tokens = bsz * seq
    n_heads = b_forget.shape[1]
    head_dim = q_norm_w.shape[1]
    assert head_dim == LANES
    attn_w = n_heads * head_dim
    ssm_w = w_glu.shape[1]
    peer_heads = peer_sub_keys.shape[1]

    mod = _adaln(cond_in, w_ada[l], b_ada[l])
    sh1, sc1, g1, sh2, sc2, g2 = jnp.split(mod, N_MOD, axis=-1)

    h = _norm_modulate(x, norm1_w[l], sh1, sc1).reshape(tokens, d)
    wi = w_in[l].astype(BF16)
    o_qk, o_v, o_f, o_u, o_g = 0, 2 * attn_w, 3 * attn_w, 3 * attn_w + n_heads, 3 * attn_w + n_heads + ssm_w
    qk_w = jnp.concatenate([jnp.tile(q_norm_w[l] * (head_dim ** -0.5 * LOG2E), n_heads),
                            jnp.tile(k_norm_w[l], n_heads)])
    qk = _project(h, wi[:, o_qk:o_v], BF16, _ep_head_rms, qk_w, name="proj_qk")
    vt = _project_transposed(h, wi[:, o_v:o_f].T, BF16, name="proj_v")
    u = _project(h, wi[:, o_u:o_g], F32, name="proj_u")
    gates = _project(h, wi[:, o_g:], BF16, _ep_sigmoid, name="proj_gates")
    w_f = jnp.zeros((d, LANES), BF16).at[:, :n_heads].set(wi[:, o_f:o_u])
    b_f = jnp.zeros((LANES,), F32).at[:n_heads].set(b_forget[l])
    log_f = _project(h, w_f, F32, _ep_log_sigmoid, b_f, name="proj_forget")
    log_f_t = log_f.reshape(bsz, seq, LANES)[:, :, :SUBLANES].transpose(0, 2, 1)
    terms = _forget_bias_terms(log_f_t)[:, :, :n_heads]
    bias = jnp.zeros((tokens, LANES), BF16).at[:, :BIAS_TERMS * n_heads].set(
        terms.transpose(0, 3, 2, 1).reshape(tokens, BIAS_TERMS * n_heads).astype(BF16))
    y_attn = _fox_attention(qk, vt, bias, n_heads, bsz, seq)

    coef_re, coef_im, bb_re, bb_im = _ssm_prep(ssm_A_re[l], ssm_A_im[l], ssm_log_dt[l], ssm_B_re[l], ssm_B_im[l])
    y_ssm = _s5_glu(u.reshape(bsz, seq, ssm_w), coef_re, coef_im, bb_re, bb_im, ssm_C_re[l], ssm_C_im[l],
                    ssm_D[l], w_glu[l], b_glu[l]).reshape(tokens, ssm_w)

    merged = _merge(y_ssm, y_attn, gates, w_ssm_up[l].astype(BF16), w_attn_up[l].astype(BF16))
    x1, h2, h2t = _out_proj(merged.reshape(bsz, seq, d), w_out[l].astype(BF16), x, g1, norm2_w[l], sh2, sc2)

    keys = peer_sub_keys[l].reshape(2 * peer_heads, peer_sub_keys.shape[3], peer_sub_keys.shape[4]).astype(BF16)
    h2f = h2.reshape(tokens, d)
    scores = _peer_scores(h2f, w_peer_q[l].astype(BF16), keys)
    peer_t = _peer_dense(h2t, peer_u[l].astype(BF16), peer_v[l].T.astype(BF16), scores, peer_heads)
    return _final_residual(x1, g2, peer_t)


def kernel(x, c, w_ada, b_ada, norm1_w, w_in, b_forget, q_norm_w, k_norm_w, ssm_A_re, ssm_A_im, ssm_log_dt,
           ssm_B_re, ssm_B_im, ssm_C_re, ssm_C_im, ssm_D, w_glu, b_glu, w_ssm_up, w_attn_up, w_out, norm2_w,
           w_peer_q, peer_sub_keys, peer_u, peer_v):
    for l in range(w_ada.shape[0]):
        x = _layer(x, c, l, w_ada, b_ada, norm1_w, w_in, b_forget, q_norm_w, k_norm_w,
                   ssm_A_re, ssm_A_im, ssm_log_dt, ssm_B_re, ssm_B_im, ssm_C_re, ssm_C_im, ssm_D,
                   w_glu, b_glu, w_ssm_up, w_attn_up, w_out, norm2_w, w_peer_q, peer_sub_keys, peer_u, peer_v)
    return x
```

```python
import functools
import math

import numpy as np
import jax
import jax.numpy as jnp
from jax import lax
from jax.experimental import pallas as pl
from jax.experimental.pallas import tpu as pltpu

F32 = jnp.float32
BF16 = jnp.bfloat16

RMS_EPS = 1e-6
MASK_VALUE = -1e30
N_MOD = 6
PEER_TOPK = 16
SQRT_HALF = math.sqrt(0.5)
LOG2E = math.log2(math.e)
BIAS_TERMS = 3

LANES = 128
SUBLANES = 8
VMEM_LIMIT_BYTES = 56 * 1024 * 1024


def _params(*sem):
    return pltpu.CompilerParams(dimension_semantics=sem, vmem_limit_bytes=VMEM_LIMIT_BYTES)


def _gelu_exact(x):
    return 0.5 * x * (1.0 + lax.erf(x * SQRT_HALF))


def _adaln_kernel(c_ref, w_ref, b_ref, o_ref):
    c = c_ref[...]
    cond = c * jax.nn.sigmoid(c)
    o_ref[...] = jnp.dot(cond.astype(BF16), w_ref[...].astype(BF16), preferred_element_type=F32) + b_ref[...]


def _adaln(c, w, b, tn=512):
    bsz, d = c.shape
    n = w.shape[1]
    c8 = jnp.zeros((SUBLANES, d), F32).at[:bsz].set(c)
    out = pl.pallas_call(
        _adaln_kernel,
        grid=(n // tn,),
        in_specs=[pl.BlockSpec((SUBLANES, d), lambda j: (0, 0)),
                  pl.BlockSpec((d, tn), lambda j: (0, j)),
                  pl.BlockSpec((1, tn), lambda j: (0, j))],
        out_specs=pl.BlockSpec((SUBLANES, tn), lambda j: (0, j)),
        out_shape=jax.ShapeDtypeStruct((SUBLANES, n), F32),
        compiler_params=_params("arbitrary"),
        name="adaln",
    )(c8, w, b.reshape(1, n))
    return out[:bsz]


def _norm_mod(x, w, shift, scale):
    xf = x * lax.rsqrt(jnp.mean(x * x, axis=-1, keepdims=True) + RMS_EPS)
    return (xf * w) * (1.0 + scale) + shift


def _norm_mod_kernel(x_ref, w_ref, sh_ref, sc_ref, o_ref):
    o_ref[...] = _norm_mod(x_ref[...], w_ref[...], sh_ref[...], sc_ref[...]).astype(o_ref.dtype)


def _norm_modulate(x3, w, shift, scale, tm=512):
    bsz, seq, d = x3.shape
    tm = min(tm, seq)
    vec = pl.BlockSpec((None, 1, d), lambda b, i: (b, 0, 0))
    return pl.pallas_call(
        _norm_mod_kernel,
        grid=(bsz, seq // tm),
        in_specs=[pl.BlockSpec((None, tm, d), lambda b, i: (b, i, 0)),
                  pl.BlockSpec((1, d), lambda b, i: (0, 0)), vec, vec],
        out_specs=pl.BlockSpec((None, tm, d), lambda b, i: (b, i, 0)),
        out_shape=jax.ShapeDtypeStruct((bsz, seq, d), BF16),
        compiler_params=_params("arbitrary", "arbitrary"),
        name="norm_modulate",
    )(x3, w.reshape(1, d), shift.reshape(bsz, 1, d), scale.reshape(bsz, 1, d))


def _ep_plain(acc):
    return acc


def _ep_sigmoid(acc):
    return jax.nn.sigmoid(acc)


def _ep_head_rms(acc, w):
    outs = []
    for s in range(0, acc.shape[1], LANES):
        t = acc[:, s:s + LANES]
        t = t * lax.rsqrt(jnp.mean(t * t, axis=-1, keepdims=True) + RMS_EPS)
        outs.append(t * w[:, s:s + LANES])
    return jnp.concatenate(outs, axis=1)


def _ep_log_sigmoid(acc, b):
    z = acc + b
    return jnp.minimum(z, 0.0) - jnp.log1p(jnp.exp(-jnp.abs(z)))


def _proj_kernel(epilogue, a_ref, w_ref, *rest):
    o_ref = rest[-1]
    acc = jnp.dot(a_ref[...], w_ref[...], preferred_element_type=F32)
    o_ref[...] = epilogue(acc, *[r[...] for r in rest[:-1]]).astype(o_ref.dtype)


def _project(a, w, out_dtype, epilogue=_ep_plain, row_vec=None, tm=1024, tn=512, name="project"):
    m, k = a.shape
    n = w.shape[1]
    tm, tn = min(tm, m), min(tn, n)
    in_specs = [pl.BlockSpec((tm, k), lambda i, j: (i, 0)), pl.BlockSpec((k, tn), lambda i, j: (0, j))]
    args = [a, w]
    if row_vec is not None:
        in_specs.append(pl.BlockSpec((1, tn), lambda i, j: (0, j)))
        args.append(row_vec.reshape(1, n).astype(F32))
    return pl.pallas_call(
        functools.partial(_proj_kernel, epilogue),
        grid=(m // tm, n // tn),
        in_specs=in_specs,
        out_specs=pl.BlockSpec((tm, tn), lambda i, j: (i, j)),
        out_shape=jax.ShapeDtypeStruct((m, n), out_dtype),
        compiler_params=_params("arbitrary", "arbitrary"),
        name=name,
    )(*args)


def _proj_t_kernel(a_ref, wt_ref, o_ref):
    o_ref[...] = lax.dot_general(wt_ref[...], a_ref[...], (((1,), (1,)), ((), ())),
                                 preferred_element_type=F32).astype(o_ref.dtype)


def _project_transposed(a, wt, out_dtype, tm=1024, tn=512, name="project_t"):
    m, k = a.shape
    n = wt.shape[0]
    tm, tn = min(tm, m), min(tn, n)
    return pl.pallas_call(
        _proj_t_kernel,
        grid=(m // tm, n // tn),
        in_specs=[pl.BlockSpec((tm, k), lambda i, j: (i, 0)), pl.BlockSpec((tn, k), lambda i, j: (j, 0))],
        out_specs=pl.BlockSpec((tn, tm), lambda i, j: (j, i)),
        out_shape=jax.ShapeDtypeStruct((n, m), out_dtype),
        compiler_params=_params("arbitrary", "arbitrary"),
        name=name,
    )(a, wt)


def _cumsum_kernel(x_ref, o_ref):
    x = x_ref[...]
    n = x.shape[1]
    lane = lax.broadcasted_iota(jnp.int32, x.shape, 1)
    k = 1
    while k < n:
        x = x + jnp.where(lane >= k, pltpu.roll(x, k, axis=1), 0.0)
        k *= 2
    g = -LOG2E * x
    hi = g.astype(BF16).astype(F32)
    mid = (g - hi).astype(BF16).astype(F32)
    o_ref[0] = hi
    o_ref[1] = mid
    o_ref[2] = (g - hi - mid).astype(BF16).astype(F32)


def _forget_bias_terms(x):
    bsz, r, n = x.shape
    return pl.pallas_call(
        _cumsum_kernel,
        grid=(bsz,),
        in_specs=[pl.BlockSpec((None, r, n), lambda b: (b, 0, 0))],
        out_specs=pl.BlockSpec((None, BIAS_TERMS, r, n), lambda b: (b, 0, 0, 0)),
        out_shape=jax.ShapeDtypeStruct((bsz, BIAS_TERMS, r, n), F32),
        compiler_params=_params("arbitrary"),
        name="forget_cumsum",
    )(x)


def _attn_kernel(n_heads, qi_ref, kj_ref, q_ref, k_ref, vt_ref, bias_ref, o_ref, m_scr, l_scr, acc_scr):
    step = pl.program_id(1)
    i = qi_ref[step]
    j = kj_ref[step]
    tq = q_ref.shape[0]
    tk = k_ref.shape[0]
    ratio = tq // tk

    @pl.when(j == 0)
    def _():
        m_scr[...] = jnp.full(m_scr.shape, MASK_VALUE, F32)
        l_scr[...] = jnp.zeros(l_scr.shape, F32)
        acc_scr[...] = jnp.zeros(acc_scr.shape, F32)

    def update(masked):
        bias = bias_ref[...]
        lane = lax.broadcasted_iota(jnp.int32, (tq, LANES), 1)
        if masked:
            key = j * tk + lax.broadcasted_iota(jnp.int32, (tk, tq), 0)
            qry = i * tq + lax.broadcasted_iota(jnp.int32, (tk, tq), 1)
            keep = key <= qry
        def scores(h):
            sl = slice(h * LANES, (h + 1) * LANES)
            pick = jnp.where((lane >= BIAS_TERMS * h) & (lane < BIAS_TERMS * (h + 1)), 1.0, 0.0).astype(BF16)
            qq = jnp.concatenate([q_ref[:, sl], pick], axis=1)
            kk = jnp.concatenate([k_ref[:, sl], bias], axis=1)
            return lax.dot_general(kk, qq, (((1,), (1,)), ((), ())), preferred_element_type=F32)

        st_next = scores(0)
        for h in range(n_heads):
            sl = slice(h * LANES, (h + 1) * LANES)
            st = st_next
            if h + 1 < n_heads:
                st_next = scores(h + 1)
            if masked:
                st = jnp.where(keep, st, MASK_VALUE)
            m_prev = m_scr[h]
            m_new = jnp.maximum(m_prev, jnp.max(st, axis=0, keepdims=True))
            alpha = jnp.exp2(m_prev - m_new)
            p = jnp.exp2(st - m_new)
            l_scr[h] = alpha * l_scr[h] + jnp.sum(p, axis=0, keepdims=True)
            acc_scr[h] = alpha * acc_scr[h] + jnp.dot(vt_ref[sl, :], p.astype(BF16), preferred_element_type=F32)
            m_scr[h] = m_new

    @pl.when(j < i * ratio)
    def _():
        update(False)

    @pl.when(j >= i * ratio)
    def _():
        update(True)

    @pl.when(j == (i + 1) * ratio - 1)
    def _():
        for h in range(n_heads):
            o_ref[:, h * LANES:(h + 1) * LANES] = (acc_scr[h] / l_scr[h]).T.astype(o_ref.dtype)


def _fox_attention(qk, vt, bias, n_heads, bsz, seq, tq=1024, tk=512):
    tq, tk = min(tq, seq), min(tk, seq)
    nq, nk = seq // tq, seq // tk
    ratio = tq // tk
    width = n_heads * LANES
    pairs = [(i, j) for i in range(nq) for j in range((i + 1) * ratio)]
    qi = jnp.asarray(np.array([p[0] for p in pairs], np.int32))
    kj = jnp.asarray(np.array([p[1] for p in pairs], np.int32))
    grid_spec = pltpu.PrefetchScalarGridSpec(
        num_scalar_prefetch=2,
        grid=(bsz, len(pairs)),
        in_specs=[
            pl.BlockSpec((tq, width), lambda b, s, qi, kj: (b * nq + qi[s], 0)),
            pl.BlockSpec((tk, width), lambda b, s, qi, kj: (b * nk + kj[s], 1)),
            pl.BlockSpec((width, tk), lambda b, s, qi, kj: (0, b * nk + kj[s])),
            pl.BlockSpec((tk, LANES), lambda b, s, qi, kj: (b * nk + kj[s], 0)),
        ],
        out_specs=pl.BlockSpec((tq, width), lambda b, s, qi, kj: (b * nq + qi[s], 0)),
        scratch_shapes=[pltpu.VMEM((n_heads, 1, tq), F32), pltpu.VMEM((n_heads, 1, tq), F32),
                        pltpu.VMEM((n_heads, LANES, tq), F32)],
    )
    return pl.pallas_call(
        functools.partial(_attn_kernel, n_heads),
        grid_spec=grid_spec,
        out_shape=jax.ShapeDtypeStruct((bsz * seq, width), BF16),
        compiler_params=_params("arbitrary", "arbitrary"),
        name="fox_attention",
    )(qi, kj, qk, qk, vt, bias)


def _ssm_prep_kernel(are_ref, aim_ref, ldt_ref, bre_ref, bim_ref, cre_ref, cim_ref, bbre_ref, bbim_ref):
    lam_re = are_ref[...]
    lam_im = aim_ref[...]
    dt = jnp.exp(ldt_ref[...])
    mag = jnp.exp(lam_re * dt)
    ab_re = mag * jnp.cos(lam_im * dt)
    ab_im = mag * jnp.sin(lam_im * dt)
    den = lam_re * lam_re + lam_im * lam_im
    nr = ab_re - 1.0
    coef_re = (nr * lam_re + ab_im * lam_im) / den
    coef_im = (ab_im * lam_re - nr * lam_im) / den
    b_re = bre_ref[...]
    b_im = bim_ref[...]
    bbre_ref[...] = coef_re * b_re - coef_im * b_im
    bbim_ref[...] = coef_re * b_im + coef_im * b_re
    pr, pi = ab_re, ab_im
    pows = []
    for _ in range(SUBLANES):
        pows.append((pr, pi))
        pr, pi = pr * ab_re - pi * ab_im, pr * ab_im + pi * ab_re
    row = lax.broadcasted_iota(jnp.int32, (SUBLANES, lam_re.shape[1]), 0)
    for idx, k in enumerate((1, 2, 4)):
        cre_ref[idx] = jnp.where(row >= k, pows[k - 1][0], 0.0)
        cim_ref[idx] = jnp.where(row >= k, pows[k - 1][1], 0.0)
    cre_ref[3] = jnp.concatenate([p[0] for p in pows], axis=0)
    cim_ref[3] = jnp.concatenate([p[1] for p in pows], axis=0)


def _ssm_prep(a_re, a_im, log_dt, b_re, b_im):
    g, p = a_re.shape
    gc = b_re.shape[2]
    gp = g * p
    flat = lambda a: a.reshape(1, gp)
    ldt = jnp.broadcast_to(log_dt[:, None], (g, p)).reshape(1, gp)
    bt = lambda b: b.transpose(2, 0, 1).reshape(gc, gp)
    return pl.pallas_call(
        _ssm_prep_kernel,
        out_shape=[jax.ShapeDtypeStruct((4, SUBLANES, gp), F32)] * 2 + [jax.ShapeDtypeStruct((gc, gp), F32)] * 2,
        name="ssm_discretize",
    )(flat(a_re), flat(a_im), ldt, bt(b_re), bt(b_im))


def _ssm_kernel(lane_chunk, u_ref, bre_ref, bim_ref, cfre_ref, cfim_ref, ctre_ref, ctim_ref, d_ref,
                wg_ref, bg_ref, o_ref, xre_scr, xim_scr, car_scr, cai_scr):
    tt = u_ref.shape[0]
    gp = xre_scr.shape[1]

    @pl.when(pl.program_id(1) == 0)
    def _():
        car_scr[...] = jnp.zeros(car_scr.shape, F32)
        cai_scr[...] = jnp.zeros(cai_scr.shape, F32)

    u = u_ref[...]
    ub = u.astype(BF16)
    xre_scr[...] = jnp.dot(ub, bre_ref[...], preferred_element_type=F32)
    xim_scr[...] = jnp.dot(ub, bim_ref[...], preferred_element_type=F32)

    for c0 in range(0, gp, lane_chunk):
        cols = slice(c0, c0 + lane_chunk)

        def body(r, carry, cols=cols):
            cr, ci = carry
            coefs = [(cfre_ref[n, :, cols], cfim_ref[n, :, cols]) for n in range(4)]
            rows = pl.ds(pl.multiple_of(r * SUBLANES, SUBLANES), SUBLANES)
            xr = xre_scr[rows, cols]
            xi = xim_scr[rows, cols]
            for n, k in enumerate((1, 2, 4)):
                ar, ai = coefs[n]
                sr = pltpu.roll(xr, k, axis=0)
                si = pltpu.roll(xi, k, axis=0)
                xr, xi = xr + ar * sr - ai * si, xi + ar * si + ai * sr
            pr, pi = coefs[3]
            xr, xi = xr + pr * cr - pi * ci, xi + pr * ci + pi * cr
            xre_scr[rows, cols] = xr
            xim_scr[rows, cols] = xi
            last = slice(SUBLANES - 1, SUBLANES)
            return (jnp.broadcast_to(xr[last, :], xr.shape), jnp.broadcast_to(xi[last, :], xi.shape))

        cr, ci = lax.fori_loop(0, tt // SUBLANES, body, (car_scr[:, cols], cai_scr[:, cols]))
        car_scr[:, cols] = cr
        cai_scr[:, cols] = ci

    y = (jnp.dot(xre_scr[...].astype(BF16), ctre_ref[...], preferred_element_type=F32)
         - jnp.dot(xim_scr[...].astype(BF16), ctim_ref[...], preferred_element_type=F32)
         + d_ref[...] * u)
    g = _gelu_exact(y)
    gate = jnp.dot(g.astype(BF16), wg_ref[...], preferred_element_type=F32) + bg_ref[...]
    o_ref[...] = (g * jax.nn.sigmoid(gate)).astype(o_ref.dtype)


def _block_diag_rows(m, groups):
    gc, gp = m.shape
    p = gp // groups
    rows = jnp.tile(m, (groups, 1))
    rg = lax.broadcasted_iota(jnp.int32, (groups * gc, gp), 0) // gc
    cg = lax.broadcasted_iota(jnp.int32, (groups * gc, gp), 1) // p
    return jnp.where(rg == cg, rows, 0.0)


def _s5_glu(u3, coef_re, coef_im, bb_re, bb_im, c_re, c_im, d_skip, w_glu, b_glu, tt=512, lane_chunk=512):
    bsz, seq, width = u3.shape
    g, gc, p = c_re.shape
    gp = g * p
    tt = min(tt, seq)
    lane_chunk = min(lane_chunk, gp)
    bd_b_re = _block_diag_rows(bb_re, g).astype(BF16)
    bd_b_im = _block_diag_rows(bb_im, g).astype(BF16)
    ct = lambda c: _block_diag_rows(c.transpose(1, 0, 2).reshape(gc, gp), g).T.astype(BF16)
    const = lambda shape: pl.BlockSpec(shape, lambda b, i: tuple(0 for _ in shape))
    return pl.pallas_call(
        functools.partial(_ssm_kernel, lane_chunk),
        grid=(bsz, seq // tt),
        in_specs=[pl.BlockSpec((None, tt, width), lambda b, i: (b, i, 0)),
                  const((width, gp)), const((width, gp)),
                  const((4, SUBLANES, gp)), const((4, SUBLANES, gp)),
                  const((gp, width)), const((gp, width)),
                  const((1, width)), const((width, width)), const((1, width))],
        out_specs=pl.BlockSpec((None, tt, width), lambda b, i: (b, i, 0)),
        out_shape=jax.ShapeDtypeStruct((bsz, seq, width), BF16),
        scratch_shapes=[pltpu.VMEM((tt, gp), F32), pltpu.VMEM((tt, gp), F32),
                        pltpu.VMEM((SUBLANES, gp), F32), pltpu.VMEM((SUBLANES, gp), F32)],
        compiler_params=_params("arbitrary", "arbitrary"),
        name="s5_glu",
    )(u3, bd_b_re, bd_b_im, coef_re, coef_im, ct(c_re), ct(c_im),
      d_skip.reshape(1, width).astype(F32), w_glu.astype(BF16), b_glu.reshape(1, width).astype(F32))


def _merge_kernel(ys_ref, ya_ref, gt_ref, wsu_ref, wau_ref, o_ref):
    d = o_ref.shape[1]
    ms = jnp.dot(ys_ref[...], wsu_ref[...], preferred_element_type=F32)
    ma = jnp.dot(ya_ref[...], wau_ref[...], preferred_element_type=F32)
    o_ref[...] = (gt_ref[:, :d].astype(F32) * ms + gt_ref[:, d:].astype(F32) * ma).astype(o_ref.dtype)


def _merge(ys, ya, gates, w_ssm_up, w_attn_up, tm=512):
    m = ys.shape[0]
    d = w_ssm_up.shape[1]
    tm = min(tm, m)
    row = lambda w: pl.BlockSpec((tm, w), lambda i: (i, 0))
    const = lambda a: pl.BlockSpec(a.shape, lambda i: (0, 0))
    return pl.pallas_call(
        _merge_kernel,
        grid=(m // tm,),
        in_specs=[row(ys.shape[1]), row(ya.shape[1]), row(gates.shape[1]), const(w_ssm_up), const(w_attn_up)],
        out_specs=row(d),
        out_shape=jax.ShapeDtypeStruct((m, d), BF16),
        compiler_params=_params("arbitrary"),
        name="gated_merge",
    )(ys, ya, gates, w_ssm_up, w_attn_up)


def _out_proj_kernel(mg_ref, w_ref, x_ref, g1_ref, nw_ref, sh_ref, sc_ref, x1_ref, h2_ref, h2t_ref):
    x1 = x_ref[...] + g1_ref[...] * jnp.dot(mg_ref[...], w_ref[...], preferred_element_type=F32)
    x1_ref[...] = x1
    h2 = _norm_mod(x1, nw_ref[...], sh_ref[...], sc_ref[...])
    h2_ref[...] = h2.astype(h2_ref.dtype)
    h2t_ref[...] = h2.T.astype(h2t_ref.dtype)


def _out_proj(merged3, w_out, x3, g1, norm_w, shift, scale, tm=512):
    bsz, seq, d = x3.shape
    tm = min(tm, seq)
    nt = seq // tm
    row = pl.BlockSpec((None, tm, d), lambda b, i: (b, i, 0))
    vec = pl.BlockSpec((None, 1, d), lambda b, i: (b, 0, 0))
    r3 = lambda a: a.reshape(bsz, 1, d)
    return pl.pallas_call(
        _out_proj_kernel,
        grid=(bsz, nt),
        in_specs=[row, pl.BlockSpec((d, d), lambda b, i: (0, 0)), row, vec,
                  pl.BlockSpec((1, d), lambda b, i: (0, 0)), vec, vec],
        out_specs=[row, row, pl.BlockSpec((d, tm), lambda b, i: (0, b * nt + i))],
        out_shape=[jax.ShapeDtypeStruct((bsz, seq, d), F32), jax.ShapeDtypeStruct((bsz, seq, d), BF16),
                   jax.ShapeDtypeStruct((d, bsz * seq), BF16)],
        compiler_params=_params("arbitrary", "arbitrary"),
        name="out_proj_residual",
    )(merged3, w_out, x3, r3(g1), norm_w.reshape(1, d), r3(shift), r3(scale))


def _peer_scores_kernel(h_ref, wq_ref, keys_ref, o_ref):
    q = jnp.dot(h_ref[...], wq_ref[...], preferred_element_type=F32).astype(BF16)
    n_hp = keys_ref.shape[0]
    nc = o_ref.shape[1]
    for hp in range(n_hp):
        dk = keys_ref.shape[2]
        st = lax.dot_general(keys_ref[hp], q[:, hp * dk:(hp + 1) * dk], (((1,), (1,)), ((), ())),
                             preferred_element_type=F32)
        for c in range(nc):
            o_ref[hp, c] = st[:, c * LANES:(c + 1) * LANES]


def _peer_scores(h2, w_q, keys, tm=512):
    m, d = h2.shape
    n_hp, n_keys, dk = keys.shape
    tm = min(tm, m)
    nc = tm // LANES
    return pl.pallas_call(
        _peer_scores_kernel,
        grid=(m // tm,),
        in_specs=[pl.BlockSpec((tm, d), lambda i: (i, 0)),
                  pl.BlockSpec(w_q.shape, lambda i: (0, 0)),
                  pl.BlockSpec(keys.shape, lambda i: (0, 0, 0))],
        out_specs=pl.BlockSpec((n_hp, nc, n_keys, LANES), lambda i: (0, i, 0, 0)),
        out_shape=jax.ShapeDtypeStruct((n_hp, m // LANES, n_keys, LANES), F32),
        compiler_params=_params("arbitrary"),
        name="peer_scores",
    )(h2, w_q, keys)


def _candidate_pairs(k):
    return [(x, y) for x in range(k) for y in range(k) if (x + 1) * (y + 1) <= k]


def _compare_exchange(xs, i, l):
    hi, lo = jnp.maximum(xs[i], xs[l]), jnp.minimum(xs[i], xs[l])
    xs[i], xs[l] = hi, lo


def _bitonic_merge_desc(xs):
    n = len(xs)
    j = n // 2
    while j >= 1:
        for i in range(n):
            if i ^ j > i:
                _compare_exchange(xs, i, i ^ j)
        j //= 2


def _sort_desc(xs):
    n = len(xs)
    k = 2
    while k <= n:
        j = k // 2
        while j >= 1:
            for i in range(n):
                l = i ^ j
                if l > i:
                    if (i & k) == 0 or k == n:
                        _compare_exchange(xs, i, l)
                    else:
                        _compare_exchange(xs, l, i)
            j //= 2
        k *= 2


def _top_sorted(s, k):
    xs = [s[v * SUBLANES:(v + 1) * SUBLANES, :] for v in range(s.shape[0] // SUBLANES)]
    assert len(xs) == k
    _sort_desc(xs)
    shift = SUBLANES // 2
    while shift >= 1:
        ys = [pltpu.roll(x, shift, axis=0) for x in xs]
        xs = [jnp.maximum(xs[v], ys[k - 1 - v]) for v in range(k)]
        _bitonic_merge_desc(xs)
        shift //= 2
    return xs


def _route_combo(sc_ref, slot, idx, t1_scr, e1_scr, e2_scr, cand_scr, nc):
    k = PEER_TOPK
    pairs = _candidate_pairs(k)
    h = idx // nc
    c = idx % nc
    s1 = sc_ref[2 * h, c]
    s2 = sc_ref[2 * h + 1, c]
    a = _top_sorted(s1, k)
    b = _top_sorted(s2, k)
    cand_scr[...] = jnp.full(cand_scr.shape, -jnp.inf, F32)
    for r, (x, y) in enumerate(pairs):
        cand_scr[r:r + 1, :] = (a[x] + b[y])[0:1, :]
    cand = cand_scr[...]
    work = cand
    remaining = jnp.full((1, LANES), float(k), F32)
    tau = jnp.full((1, LANES), -jnp.inf, F32)
    for _ in range(k):
        m = jnp.max(work, axis=0, keepdims=True)
        eq = work == m
        after = remaining - jnp.sum(jnp.where(eq, 1.0, 0.0), axis=0, keepdims=True)
        tau = jnp.where((remaining > 0.0) & (after <= 0.0), m, tau)
        remaining = after
        work = jnp.where(eq, -jnp.inf, work)
    top = (a[0] + b[0])[0:1, :]
    z = jnp.sum(jnp.where(cand >= tau, jnp.exp(cand - top), 0.0), axis=0, keepdims=True)
    tau_b = jnp.broadcast_to(tau, (SUBLANES, LANES))
    t1 = jnp.full(s1.shape, jnp.inf, F32)
    for x in reversed(range(k)):
        t1x = jnp.full((SUBLANES, LANES), jnp.inf, F32)
        for y in range(k // (x + 1)):
            t1x = jnp.where(a[x] + b[y] >= tau_b, b[y], t1x)
        t1 = jnp.where(s1 == a[x][0:1, :], t1x[0:1, :], t1)
    t1_scr[slot, h, c] = t1
    e1_scr[slot, h, c] = jnp.exp(s1 - a[0][0:1, :]) / z
    e2_scr[slot, h, c] = jnp.exp(s2 - b[0][0:1, :])


def _peer_dense_kernel(n_heads, ne, ht_ref, u_ref, vt_ref, sc_ref, scn_ref, ot_ref,
                       t1_scr, e1_scr, e2_scr, cand_scr, at_scr, g_scr):
    s = pl.program_id(0)
    et = u_ref.shape[0]
    tm = ht_ref.shape[1]
    nc = tm // LANES
    n_keys = sc_ref.shape[2]
    n_combos = n_heads * nc
    g_step = jnp.maximum(s - 1, 0)
    tile = g_step // ne
    e = g_step % ne
    slot = tile % 2
    at_w = s % 2
    at_r = 1 - at_w

    @pl.when(s == 0)
    def _():
        at_scr[...] = jnp.zeros(at_scr.shape, F32)

        def first_tile(idx, carry):
            _route_combo(sc_ref, 0, idx, t1_scr, e1_scr, e2_scr, cand_scr, nc)
            return carry

        lax.fori_loop(0, n_combos, first_tile, 0)

    @pl.when(e == 0)
    def _():
        ot_ref[...] = jnp.zeros(ot_ref.shape, F32)

    at_scr[at_w] = jnp.dot(u_ref[...], ht_ref[...], preferred_element_type=F32)

    per_step = -(-n_combos // ne)
    for r in range(per_step):
        idx = jnp.minimum(e * per_step + r, n_combos - 1)
        _route_combo(scn_ref, 1 - slot, idx, t1_scr, e1_scr, e2_scr, cand_scr, nc)

    half = tm // 2
    for n in range(2):
        for cc in range(half // LANES):
            c = n * (half // LANES) + cc
            for ib in range(et // n_keys):
                i = e * (et // n_keys) + ib
                w = jnp.zeros((n_keys, LANES), F32)
                for h in range(n_heads):
                    t1 = t1_scr[slot, h, c, pl.ds(i, 1), :]
                    e1 = e1_scr[slot, h, c, pl.ds(i, 1), :]
                    w = w + jnp.where(sc_ref[2 * h + 1, c] >= t1, e2_scr[slot, h, c], 0.0) * e1
                a_blk = at_scr[at_r, ib * n_keys:(ib + 1) * n_keys, c * LANES:(c + 1) * LANES]
                g_scr[ib * n_keys:(ib + 1) * n_keys, c * LANES:(c + 1) * LANES] = (
                    _gelu_exact(a_blk) * w).astype(BF16)
        cols = slice(n * half, (n + 1) * half)
        ot_ref[:, cols] += jnp.dot(vt_ref[...], g_scr[:, cols], preferred_element_type=F32)


def _peer_dense(h2t, u_tab, vt_tab, scores, n_heads, tm=512, et=1024):
    d, m = h2t.shape
    n_exp = u_tab.shape[0]
    n_hp, _, n_keys, _ = scores.shape
    tm = min(tm, m)
    nc = tm // LANES
    ne = n_exp // et
    n_tiles = m // tm
    total = n_tiles * ne
    n_cand = -(-len(_candidate_pairs(PEER_TOPK)) // SUBLANES) * SUBLANES
    per_tok = pltpu.VMEM((2, n_heads, nc, n_keys, LANES), F32)
    act = lambda s: jnp.minimum(s, total - 1)
    gat = lambda s: jnp.maximum(s - 1, 0)
    once = dict(pipeline_mode=pl.Buffered(1))
    return pl.pallas_call(
        functools.partial(_peer_dense_kernel, n_heads, ne),
        grid=(total + 1,),
        in_specs=[pl.BlockSpec((d, tm), lambda s: (0, act(s) // ne)),
                  pl.BlockSpec((et, d), lambda s: (act(s) % ne, 0)),
                  pl.BlockSpec((d, et), lambda s: (0, gat(s) % ne)),
                  pl.BlockSpec((n_hp, nc, n_keys, LANES), lambda s: (0, gat(s) // ne, 0, 0), **once),
                  pl.BlockSpec((n_hp, nc, n_keys, LANES),
                               lambda s: (0, jnp.minimum(gat(s) // ne + 1, n_tiles - 1), 0, 0), **once)],
        out_specs=pl.BlockSpec((d, tm), lambda s: (0, gat(s) // ne), **once),
        out_shape=jax.ShapeDtypeStruct((d, m), F32),
        scratch_shapes=[per_tok, per_tok, per_tok,
                        pltpu.VMEM((n_cand, LANES), F32),
                        pltpu.VMEM((2, et, tm), F32), pltpu.VMEM((et, tm), BF16)],
        compiler_params=_params("arbitrary"),
        name="peer_dense",
    )(h2t, u_tab, vt_tab, scores, scores)


def _final_kernel(x_ref, g_ref, pt_ref, o_ref):
    o_ref[...] = x_ref[...] + g_ref[...] * pt_ref[...].T


def _final_residual(x3, g2, peer_t, tm=512):
    bsz, seq, d = x3.shape
    tm = min(tm, seq)
    nt = seq // tm
    row = pl.BlockSpec((None, tm, d), lambda b, i: (b, i, 0))
    return pl.pallas_call(
        _final_kernel,
        grid=(bsz, nt),
        in_specs=[row, pl.BlockSpec((None, 1, d), lambda b, i: (b, 0, 0)),
                  pl.BlockSpec((d, tm), lambda b, i: (0, b * nt + i))],
        out_specs=row,
        out_shape=jax.ShapeDtypeStruct((bsz, seq, d), F32),
        compiler_params=_params("arbitrary", "arbitrary"),
        name="final_residual",
    )(x3, g2.reshape(bsz, 1, d), peer_t)


def _layer(x, cond_in, l, w_ada, b_ada, norm1_w, w_in, b_forget, q_norm_w, k_norm_w,
           ssm_A_re, ssm_A_im, ssm_log_dt, ssm_B_re, ssm_B_im, ssm_C_re, ssm_C_im, ssm_D,
           w_glu, b_glu, w_ssm_up, w_attn_up, w_out, norm2_w, w_peer_q, peer_sub_keys, peer_u, peer_v):
    bsz, seq, d = x.shape
    tokens = bsz * seq
    n_heads = b_forget.shape[1]
    head_dim = q_norm_w.shape[1]
    assert head_dim == LANES
    attn_w = n_heads * head_dim
    ssm_w = w_glu.shape[1]
    peer_heads = peer_sub_keys.shape[1]

    mod = _adaln(cond_in, w_ada[l], b_ada[l])
    sh1, sc1, g1, sh2, sc2, g2 = jnp.split(mod, N_MOD, axis=-1)

    h = _norm_modulate(x, norm1_w[l], sh1, sc1).reshape(tokens, d)
    wi = w_in[l].astype(BF16)
    o_qk, o_v, o_f, o_u, o_g = 0, 2 * attn_w, 3 * attn_w, 3 * attn_w + n_heads, 3 * attn_w + n_heads + ssm_w
    qk_w = jnp.concatenate([jnp.tile(q_norm_w[l] * (head_dim ** -0.5 * LOG2E), n_heads),
                            jnp.tile(k_norm_w[l], n_heads)])
    qk = _project(h, wi[:, o_qk:o_v], BF16, _ep_head_rms, qk_w, name="proj_qk")
    vt = _project_transposed(h, wi[:, o_v:o_f].T, BF16, name="proj_v")
    u = _project(h, wi[:, o_u:o_g], F32, name="proj_u")
    gates = _project(h, wi[:, o_g:], BF16, _ep_sigmoid, name="proj_gates")
    w_f = jnp.zeros((d, LANES), BF16).at[:, :n_heads].set(wi[:, o_f:o_u])
    b_f = jnp.zeros((LANES,), F32).at[:n_heads].set(b_forget[l])
    log_f = _project(h, w_f, F32, _ep_log_sigmoid, b_f, name="proj_forget")
    log_f_t = log_f.reshape(bsz, seq, LANES)[:, :, :SUBLANES].transpose(0, 2, 1)
    terms = _forget_bias_terms(log_f_t)[:, :, :n_heads]
    bias = jnp.zeros((tokens, LANES), BF16).at[:, :BIAS_TERMS * n_heads].set(
        terms.transpose(0, 3, 2, 1).reshape(tokens, BIAS_TERMS * n_heads).astype(BF16))
    y_attn = _fox_attention(qk, vt, bias, n_heads, bsz, seq)

    coef_re, coef_im, bb_re, bb_im = _ssm_prep(ssm_A_re[l], ssm_A_im[l], ssm_log_dt[l], ssm_B_re[l], ssm_B_im[l])
    y_ssm = _s5_glu(u.reshape(bsz, seq, ssm_w), coef_re, coef_im, bb_re, bb_im, ssm_C_re[l], ssm_C_im[l],
                    ssm_D[l], w_glu[l], b_glu[l]).reshape(tokens, ssm_w)

    merged = _merge(y_ssm, y_attn, gates, w_ssm_up[l].astype(BF16), w_attn_up[l].astype(BF16))
    x1, h2, h2t = _out_proj(merged.reshape(bsz, seq, d), w_out[l].astype(BF16), x, g1, norm2_w[l], sh2, sc2)

    keys = peer_sub_keys[l].reshape(2 * peer_heads, peer_sub_keys.shape[3], peer_sub_keys.shape[4]).astype(BF16)
    h2f = h2.reshape(tokens, d)
    scores = _peer_scores(h2f, w_peer_q[l].astype(BF16), keys)
    peer_t = _peer_dense(h2t, peer_u[l].astype(BF16), peer_v[l].T.astype(BF16), scores, peer_heads)
    return _final_residual(x1, g2, peer_t)


def kernel(x, c, w_ada, b_ada, norm1_w, w_in, b_forget, q_norm_w, k_norm_w, ssm_A_re, ssm_A_im, ssm_log_dt,
           ssm_B_re, ssm_B_im, ssm_C_re, ssm_C_im, ssm_D, w_glu, b_glu, w_ssm_up, w_attn_up, w_out, norm2_w,
           w_peer_q, peer_sub_keys, peer_u, peer_v):
    for l in range(w_ada.shape[0]):
        x = _layer(x, c, l, w_ada, b_ada, norm1_w, w_in, b_forget, q_norm_w, k_norm_w,
                   ssm_A_re, ssm_A_im, ssm_log_dt, ssm_B_re, ssm_B_im, ssm_C_re, ssm_C_im, ssm_D,
                   w_glu, b_glu, w_ssm_up, w_attn_up, w_out, norm2_w, w_peer_q, peer_sub_keys, peer_u, peer_v)
    return x
```

```python
import functools
import math

import numpy as np
import jax
import jax.numpy as jnp
from jax import lax
from jax.experimental import pallas as pl
from jax.experimental.pallas import tpu as pltpu

F32 = jnp.float32
BF16 = jnp.bfloat16

RMS_EPS = 1e-6
MASK_VALUE = -1e30
N_MOD = 6
PEER_TOPK = 16
SQRT_HALF = math.sqrt(0.5)
LOG2E = math.log2(math.e)
BIAS_TERMS = 3
DENOM_ROWS = 16

LANES = 128
SUBLANES = 8
MXU_DIM = 256
VMEM_LIMIT_BYTES = 56 * 1024 * 1024


def _params(*sem):
    return pltpu.CompilerParams(dimension_semantics=sem, vmem_limit_bytes=VMEM_LIMIT_BYTES)


def _gelu_exact(x):
    return 0.5 * x * (1.0 + lax.erf(x * SQRT_HALF))


def _adaln_kernel(c_ref, w_ref, b_ref, o_ref):
    c = c_ref[...]
    cond = c * jax.nn.sigmoid(c)
    o_ref[...] = jnp.dot(cond.astype(BF16), w_ref[...].astype(BF16), preferred_element_type=F32) + b_ref[...]


def _adaln(c, w, b, tn=512):
    bsz, d = c.shape
    n = w.shape[1]
    c8 = jnp.zeros((SUBLANES, d), F32).at[:bsz].set(c)
    out = pl.pallas_call(
        _adaln_kernel,
        grid=(n // tn,),
        in_specs=[pl.BlockSpec((SUBLANES, d), lambda j: (0, 0)),
                  pl.BlockSpec((d, tn), lambda j: (0, j)),
                  pl.BlockSpec((1, tn), lambda j: (0, j))],
        out_specs=pl.BlockSpec((SUBLANES, tn), lambda j: (0, j)),
        out_shape=jax.ShapeDtypeStruct((SUBLANES, n), F32),
        compiler_params=_params("arbitrary"),
        name="adaln",
    )(c8, w, b.reshape(1, n))
    return out[:bsz]


def _norm_mod(x, w, shift, scale):
    xf = x * lax.rsqrt(jnp.mean(x * x, axis=-1, keepdims=True) + RMS_EPS)
    return (xf * w) * (1.0 + scale) + shift


def _norm_mod_kernel(x_ref, w_ref, sh_ref, sc_ref, o_ref):
    o_ref[...] = _norm_mod(x_ref[...], w_ref[...], sh_ref[...], sc_ref[...]).astype(o_ref.dtype)


def _norm_modulate(x3, w, shift, scale, tm=512):
    bsz, seq, d = x3.shape
    tm = min(tm, seq)
    vec = pl.BlockSpec((None, 1, d), lambda b, i: (b, 0, 0))
    return pl.pallas_call(
        _norm_mod_kernel,
        grid=(bsz, seq // tm),
        in_specs=[pl.BlockSpec((None, tm, d), lambda b, i: (b, i, 0)),
                  pl.BlockSpec((1, d), lambda b, i: (0, 0)), vec, vec],
        out_specs=pl.BlockSpec((None, tm, d), lambda b, i: (b, i, 0)),
        out_shape=jax.ShapeDtypeStruct((bsz, seq, d), BF16),
        compiler_params=_params("arbitrary", "arbitrary"),
        name="norm_modulate",
    )(x3, w.reshape(1, d), shift.reshape(bsz, 1, d), scale.reshape(bsz, 1, d))


def _ep_plain(acc):
    return acc


def _ep_sigmoid(acc):
    return jax.nn.sigmoid(acc)


def _ep_head_rms(acc, w):
    outs = []
    for s in range(0, acc.shape[1], LANES):
        t = acc[:, s:s + LANES]
        t = t * lax.rsqrt(jnp.mean(t * t, axis=-1, keepdims=True) + RMS_EPS)
        outs.append(t * w[:, s:s + LANES])
    return jnp.concatenate(outs, axis=1)


def _ep_log_sigmoid(acc, b):
    z = acc + b
    return jnp.minimum(z, 0.0) - jnp.log1p(jnp.exp(-jnp.abs(z)))


def _proj_kernel(epilogue, a_ref, w_ref, *rest):
    o_ref = rest[-1]
    acc = jnp.dot(a_ref[...], w_ref[...], preferred_element_type=F32)
    o_ref[...] = epilogue(acc, *[r[...] for r in rest[:-1]]).astype(o_ref.dtype)


def _project(a, w, out_dtype, epilogue=_ep_plain, row_vec=None, tm=1024, tn=512, name="project"):
    m, k = a.shape
    n = w.shape[1]
    tm, tn = min(tm, m), min(tn, n)
    in_specs = [pl.BlockSpec((tm, k), lambda i, j: (i, 0)), pl.BlockSpec((k, tn), lambda i, j: (0, j))]
    args = [a, w]
    if row_vec is not None:
        in_specs.append(pl.BlockSpec((1, tn), lambda i, j: (0, j)))
        args.append(row_vec.reshape(1, n).astype(F32))
    return pl.pallas_call(
        functools.partial(_proj_kernel, epilogue),
        grid=(m // tm, n // tn),
        in_specs=in_specs,
        out_specs=pl.BlockSpec((tm, tn), lambda i, j: (i, j)),
        out_shape=jax.ShapeDtypeStruct((m, n), out_dtype),
        compiler_params=_params("arbitrary", "arbitrary"),
        name=name,
    )(*args)


def _proj_t_kernel(a_ref, wt_ref, o_ref):
    o_ref[...] = lax.dot_general(wt_ref[...], a_ref[...], (((1,), (1,)), ((), ())),
                                 preferred_element_type=F32).astype(o_ref.dtype)


def _project_transposed(a, wt, out_dtype, tm=1024, tn=512, name="project_t"):
    m, k = a.shape
    n = wt.shape[0]
    tm, tn = min(tm, m), min(tn, n)
    return pl.pallas_call(
        _proj_t_kernel,
        grid=(m // tm, n // tn),
        in_specs=[pl.BlockSpec((tm, k), lambda i, j: (i, 0)), pl.BlockSpec((tn, k), lambda i, j: (j, 0))],
        out_specs=pl.BlockSpec((tn, tm), lambda i, j: (j, i)),
        out_shape=jax.ShapeDtypeStruct((n, m), out_dtype),
        compiler_params=_params("arbitrary", "arbitrary"),
        name=name,
    )(a, wt)


def _cumsum_kernel(x_ref, o_ref):
    x = x_ref[...]
    n = x.shape[1]
    lane = lax.broadcasted_iota(jnp.int32, x.shape, 1)
    k = 1
    while k < n:
        x = x + jnp.where(lane >= k, pltpu.roll(x, k, axis=1), 0.0)
        k *= 2
    g = -LOG2E * x
    hi = g.astype(BF16).astype(F32)
    mid = (g - hi).astype(BF16).astype(F32)
    o_ref[0] = hi
    o_ref[1] = mid
    o_ref[2] = (g - hi - mid).astype(BF16).astype(F32)


def _forget_bias_terms(x):
    bsz, r, n = x.shape
    return pl.pallas_call(
        _cumsum_kernel,
        grid=(bsz,),
        in_specs=[pl.BlockSpec((None, r, n), lambda b: (b, 0, 0))],
        out_specs=pl.BlockSpec((None, BIAS_TERMS, r, n), lambda b: (b, 0, 0, 0)),
        out_shape=jax.ShapeDtypeStruct((bsz, BIAS_TERMS, r, n), F32),
        compiler_params=_params("arbitrary"),
        name="forget_cumsum",
    )(x)


def _attn_kernel(n_heads, qi_ref, kj_ref, q_ref, k_ref, vt_ref, bias_ref, o_ref, m_scr, acc_scr):
    step = pl.program_id(1)
    i = qi_ref[step]
    j = kj_ref[step]
    tq = q_ref.shape[0]
    tk = k_ref.shape[0]
    ratio = tq // tk

    @pl.when(j == 0)
    def _():
        m_scr[...] = jnp.full(m_scr.shape, MASK_VALUE, F32)
        acc_scr[...] = jnp.zeros(acc_scr.shape, F32)

    def update(masked):
        bias = bias_ref[...]
        ones_rows = jnp.where(lax.broadcasted_iota(jnp.int32, (DENOM_ROWS, tk), 0) == 0, 1.0, 0.0).astype(BF16)
        lane = lax.broadcasted_iota(jnp.int32, (tq, LANES), 1)
        if masked:
            key = j * tk + lax.broadcasted_iota(jnp.int32, (tk, tq), 0)
            qry = i * tq + lax.broadcasted_iota(jnp.int32, (tk, tq), 1)
            keep = key <= qry
        def scores(h):
            sl = slice(h * LANES, (h + 1) * LANES)
            pick = jnp.where((lane >= BIAS_TERMS * h) & (lane < BIAS_TERMS * (h + 1)), 1.0, 0.0).astype(BF16)
            qq = jnp.concatenate([q_ref[:, sl], pick], axis=1)
            kk = jnp.concatenate([k_ref[:, sl], bias], axis=1)
            return lax.dot_general(kk, qq, (((1,), (1,)), ((), ())), preferred_element_type=F32)

        st_next = scores(0)
        for h in range(n_heads):
            sl = slice(h * LANES, (h + 1) * LANES)
            st = st_next
            if h + 1 < n_heads:
                st_next = scores(h + 1)
            if masked:
                st = jnp.where(keep, st, MASK_VALUE)
            m_prev = m_scr[h]
            m_new = jnp.maximum(m_prev, jnp.max(st, axis=0, keepdims=True))
            alpha = jnp.exp2(m_prev - m_new)
            p = jnp.exp2(st - m_new)
            v_ext = jnp.concatenate([vt_ref[sl, :], ones_rows], axis=0)
            acc_scr[h] = alpha * acc_scr[h] + jnp.dot(v_ext, p.astype(BF16), preferred_element_type=F32)
            m_scr[h] = m_new

    @pl.when(j < i * ratio)
    def _():
        update(False)

    @pl.when(j >= i * ratio)
    def _():
        update(True)

    @pl.when(j == (i + 1) * ratio - 1)
    def _():
        for h in range(n_heads):
            acc = acc_scr[h]
            o_ref[:, h * LANES:(h + 1) * LANES] = (acc[:LANES] / acc[LANES:LANES + 1]).T.astype(o_ref.dtype)


def _fox_attention(qk, vt, bias, n_heads, bsz, seq, tq=1024, tk=512):
    tq, tk = min(tq, seq), min(tk, seq)
    nq, nk = seq // tq, seq // tk
    ratio = tq // tk
    width = n_heads * LANES
    pairs = [(i, j) for i in range(nq) for j in range((i + 1) * ratio)]
    qi = jnp.asarray(np.array([p[0] for p in pairs], np.int32))
    kj = jnp.asarray(np.array([p[1] for p in pairs], np.int32))
    grid_spec = pltpu.PrefetchScalarGridSpec(
        num_scalar_prefetch=2,
        grid=(bsz, len(pairs)),
        in_specs=[
            pl.BlockSpec((tq, width), lambda b, s, qi, kj: (b * nq + qi[s], 0)),
            pl.BlockSpec((tk, width), lambda b, s, qi, kj: (b * nk + kj[s], 1)),
            pl.BlockSpec((width, tk), lambda b, s, qi, kj: (0, b * nk + kj[s])),
            pl.BlockSpec((tk, LANES), lambda b, s, qi, kj: (b * nk + kj[s], 0)),
        ],
        out_specs=pl.BlockSpec((tq, width), lambda b, s, qi, kj: (b * nq + qi[s], 0)),
        scratch_shapes=[pltpu.VMEM((n_heads, 1, tq), F32), pltpu.VMEM((n_heads, LANES + DENOM_ROWS, tq), F32)],
    )
    return pl.pallas_call(
        functools.partial(_attn_kernel, n_heads),
        grid_spec=grid_spec,
        out_shape=jax.ShapeDtypeStruct((bsz * seq, width), BF16),
        compiler_params=_params("arbitrary", "arbitrary"),
        name="fox_attention",
    )(qi, kj, qk, qk, vt, bias)


def _ssm_prep_kernel(are_ref, aim_ref, ldt_ref, bre_ref, bim_ref, cre_ref, cim_ref, bbre_ref, bbim_ref):
    lam_re = are_ref[...]
    lam_im = aim_ref[...]
    dt = jnp.exp(ldt_ref[...])
    mag = jnp.exp(lam_re * dt)
    ab_re = mag * jnp.cos(lam_im * dt)
    ab_im = mag * jnp.sin(lam_im * dt)
    den = lam_re * lam_re + lam_im * lam_im
    nr = ab_re - 1.0
    coef_re = (nr * lam_re + ab_im * lam_im) / den
    coef_im = (ab_im * lam_re - nr * lam_im) / den
    b_re = bre_ref[...]
    b_im = bim_ref[...]
    bbre_ref[...] = coef_re * b_re - coef_im * b_im
    bbim_ref[...] = coef_re * b_im + coef_im * b_re
    pr, pi = ab_re, ab_im
    pows = []
    for _ in range(SUBLANES):
        pows.append((pr, pi))
        pr, pi = pr * ab_re - pi * ab_im, pr * ab_im + pi * ab_re
    row = lax.broadcasted_iota(jnp.int32, (SUBLANES, lam_re.shape[1]), 0)
    for idx, k in enumerate((1, 2, 4)):
        cre_ref[idx] = jnp.where(row >= k, pows[k - 1][0], 0.0)
        cim_ref[idx] = jnp.where(row >= k, pows[k - 1][1], 0.0)
    cre_ref[3] = jnp.concatenate([p[0] for p in pows], axis=0)
    cim_ref[3] = jnp.concatenate([p[1] for p in pows], axis=0)


def _ssm_prep(a_re, a_im, log_dt, b_re, b_im):
    g, p = a_re.shape
    gc = b_re.shape[2]
    gp = g * p
    flat = lambda a: a.reshape(1, gp)
    ldt = jnp.broadcast_to(log_dt[:, None], (g, p)).reshape(1, gp)
    bt = lambda b: b.transpose(2, 0, 1).reshape(gc, gp)
    return pl.pallas_call(
        _ssm_prep_kernel,
        out_shape=[jax.ShapeDtypeStruct((4, SUBLANES, gp), F32)] * 2 + [jax.ShapeDtypeStruct((gc, gp), F32)] * 2,
        name="ssm_discretize",
    )(flat(a_re), flat(a_im), ldt, bt(b_re), bt(b_im))


def _ssm_kernel(lane_chunk, u_ref, bre_ref, bim_ref, cfre_ref, cfim_ref, ctre_ref, ctim_ref, d_ref,
                wg_ref, bg_ref, o_ref, xre_scr, xim_scr, car_scr, cai_scr):
    tt = u_ref.shape[0]
    gp = xre_scr.shape[1]

    @pl.when(pl.program_id(1) == 0)
    def _():
        car_scr[...] = jnp.zeros(car_scr.shape, F32)
        cai_scr[...] = jnp.zeros(cai_scr.shape, F32)

    u = u_ref[...]
    ub = u.astype(BF16)
    n_chunks, cw, sw = bre_ref.shape
    for k in range(n_chunks):
        uk = ub[:, k * cw:(k + 1) * cw]
        xre_scr[:, k * sw:(k + 1) * sw] = jnp.dot(uk, bre_ref[k], preferred_element_type=F32)
        xim_scr[:, k * sw:(k + 1) * sw] = jnp.dot(uk, bim_ref[k], preferred_element_type=F32)

    for c0 in range(0, gp, lane_chunk):
        cols = slice(c0, c0 + lane_chunk)

        def body(r, carry, cols=cols):
            cr, ci = carry
            coefs = [(cfre_ref[n, :, cols], cfim_ref[n, :, cols]) for n in range(4)]
            rows = pl.ds(pl.multiple_of(r * SUBLANES, SUBLANES), SUBLANES)
            xr = xre_scr[rows, cols]
            xi = xim_scr[rows, cols]
            for n, k in enumerate((1, 2, 4)):
                ar, ai = coefs[n]
                sr = pltpu.roll(xr, k, axis=0)
                si = pltpu.roll(xi, k, axis=0)
                xr, xi = xr + ar * sr - ai * si, xi + ar * si + ai * sr
            pr, pi = coefs[3]
            xr, xi = xr + pr * cr - pi * ci, xi + pr * ci + pi * cr
            xre_scr[rows, cols] = xr
            xim_scr[rows, cols] = xi
            last = slice(SUBLANES - 1, SUBLANES)
            return (jnp.broadcast_to(xr[last, :], xr.shape), jnp.broadcast_to(xi[last, :], xi.shape))

        cr, ci = lax.fori_loop(0, tt // SUBLANES, body, (car_scr[:, cols], cai_scr[:, cols]))
        car_scr[:, cols] = cr
        cai_scr[:, cols] = ci

    cx = jnp.concatenate(
        [jnp.dot(xre_scr[:, k * sw:(k + 1) * sw].astype(BF16), ctre_ref[k], preferred_element_type=F32)
         - jnp.dot(xim_scr[:, k * sw:(k + 1) * sw].astype(BF16), ctim_ref[k], preferred_element_type=F32)
         for k in range(n_chunks)], axis=1)
    y = cx + d_ref[...] * u
    g = _gelu_exact(y)
    gate = jnp.dot(g.astype(BF16), wg_ref[...], preferred_element_type=F32) + bg_ref[...]
    o_ref[...] = (g * jax.nn.sigmoid(gate)).astype(o_ref.dtype)


def _block_diag_rows(m, groups):
    gc, gp = m.shape
    p = gp // groups
    rows = jnp.tile(m, (groups, 1))
    rg = lax.broadcasted_iota(jnp.int32, (groups * gc, gp), 0) // gc
    cg = lax.broadcasted_iota(jnp.int32, (groups * gc, gp), 1) // p
    return jnp.where(rg == cg, rows, 0.0)


def _s5_glu(u3, coef_re, coef_im, bb_re, bb_im, c_re, c_im, d_skip, w_glu, b_glu, tt=512, lane_chunk=512):
    bsz, seq, width = u3.shape
    g, gc, p = c_re.shape
    gp = g * p
    tt = min(tt, seq)
    lane_chunk = min(lane_chunk, gp)
    cw = min(MXU_DIM, width)
    n_chunks = width // cw
    sw = gp // n_chunks

    def diag_chunks(m):
        return jnp.stack([m[k * cw:(k + 1) * cw, k * sw:(k + 1) * sw] for k in range(n_chunks)])

    bd_b_re = diag_chunks(_block_diag_rows(bb_re, g)).astype(BF16)
    bd_b_im = diag_chunks(_block_diag_rows(bb_im, g)).astype(BF16)
    ct = lambda c: diag_chunks(_block_diag_rows(c.transpose(1, 0, 2).reshape(gc, gp), g)
                               ).transpose(0, 2, 1).astype(BF16)
    const = lambda shape: pl.BlockSpec(shape, lambda b, i: tuple(0 for _ in shape))
    return pl.pallas_call(
        functools.partial(_ssm_kernel, lane_chunk),
        grid=(bsz, seq // tt),
        in_specs=[pl.BlockSpec((None, tt, width), lambda b, i: (b, i, 0)),
                  const((n_chunks, cw, sw)), const((n_chunks, cw, sw)),
                  const((4, SUBLANES, gp)), const((4, SUBLANES, gp)),
                  const((n_chunks, sw, cw)), const((n_chunks, sw, cw)),
                  const((1, width)), const((width, width)), const((1, width))],
        out_specs=pl.BlockSpec((None, tt, width), lambda b, i: (b, i, 0)),
        out_shape=jax.ShapeDtypeStruct((bsz, seq, width), BF16),
        scratch_shapes=[pltpu.VMEM((tt, gp), F32), pltpu.VMEM((tt, gp), F32),
                        pltpu.VMEM((SUBLANES, gp), F32), pltpu.VMEM((SUBLANES, gp), F32)],
        compiler_params=_params("arbitrary", "arbitrary"),
        name="s5_glu",
    )(u3, bd_b_re, bd_b_im, coef_re, coef_im, ct(c_re), ct(c_im),
      d_skip.reshape(1, width).astype(F32), w_glu.astype(BF16), b_glu.reshape(1, width).astype(F32))


def _merge_kernel(ys_ref, ya_ref, gt_ref, wsu_ref, wau_ref, o_ref):
    d = o_ref.shape[1]
    ms = jnp.dot(ys_ref[...], wsu_ref[...], preferred_element_type=F32)
    ma = jnp.dot(ya_ref[...], wau_ref[...], preferred_element_type=F32)
    o_ref[...] = (gt_ref[:, :d].astype(F32) * ms + gt_ref[:, d:].astype(F32) * ma).astype(o_ref.dtype)


def _merge(ys, ya, gates, w_ssm_up, w_attn_up, tm=512):
    m = ys.shape[0]
    d = w_ssm_up.shape[1]
    tm = min(tm, m)
    row = lambda w: pl.BlockSpec((tm, w), lambda i: (i, 0))
    const = lambda a: pl.BlockSpec(a.shape, lambda i: (0, 0))
    return pl.pallas_call(
        _merge_kernel,
        grid=(m // tm,),
        in_specs=[row(ys.shape[1]), row(ya.shape[1]), row(gates.shape[1]), const(w_ssm_up), const(w_attn_up)],
        out_specs=row(d),
        out_shape=jax.ShapeDtypeStruct((m, d), BF16),
        compiler_params=_params("arbitrary"),
        name="gated_merge",
    )(ys, ya, gates, w_ssm_up, w_attn_up)


def _out_proj_kernel(mg_ref, w_ref, x_ref, g1_ref, nw_ref, sh_ref, sc_ref, x1_ref, h2_ref, h2t_ref):
    x1 = x_ref[...] + g1_ref[...] * jnp.dot(mg_ref[...], w_ref[...], preferred_element_type=F32)
    x1_ref[...] = x1
    h2 = _norm_mod(x1, nw_ref[...], sh_ref[...], sc_ref[...])
    h2_ref[...] = h2.astype(h2_ref.dtype)
    h2t_ref[...] = h2.T.astype(h2t_ref.dtype)


def _out_proj(merged3, w_out, x3, g1, norm_w, shift, scale, tm=512):
    bsz, seq, d = x3.shape
    tm = min(tm, seq)
    nt = seq // tm
    row = pl.BlockSpec((None, tm, d), lambda b, i: (b, i, 0))
    vec = pl.BlockSpec((None, 1, d), lambda b, i: (b, 0, 0))
    r3 = lambda a: a.reshape(bsz, 1, d)
    return pl.pallas_call(
        _out_proj_kernel,
        grid=(bsz, nt),
        in_specs=[row, pl.BlockSpec((d, d), lambda b, i: (0, 0)), row, vec,
                  pl.BlockSpec((1, d), lambda b, i: (0, 0)), vec, vec],
        out_specs=[row, row, pl.BlockSpec((d, tm), lambda b, i: (0, b * nt + i))],
        out_shape=[jax.ShapeDtypeStruct((bsz, seq, d), F32), jax.ShapeDtypeStruct((bsz, seq, d), BF16),
                   jax.ShapeDtypeStruct((d, bsz * seq), BF16)],
        compiler_params=_params("arbitrary", "arbitrary"),
        name="out_proj_residual",
    )(merged3, w_out, x3, r3(g1), norm_w.reshape(1, d), r3(shift), r3(scale))


def _peer_scores_kernel(h_ref, wq_ref, keys_ref, o_ref):
    q = jnp.dot(h_ref[...], wq_ref[...], preferred_element_type=F32).astype(BF16)
    n_hp = keys_ref.shape[0]
    nc = o_ref.shape[1]
    for hp in range(n_hp):
        dk = keys_ref.shape[2]
        st = lax.dot_general(keys_ref[hp], q[:, hp * dk:(hp + 1) * dk], (((1,), (1,)), ((), ())),
                             preferred_element_type=F32)
        for c in range(nc):
            o_ref[hp, c] = st[:, c * LANES:(c + 1) * LANES]


def _peer_scores(h2, w_q, keys, tm=512):
    m, d = h2.shape
    n_hp, n_keys, dk = keys.shape
    tm = min(tm, m)
    nc = tm // LANES
    return pl.pallas_call(
        _peer_scores_kernel,
        grid=(m // tm,),
        in_specs=[pl.BlockSpec((tm, d), lambda i: (i, 0)),
                  pl.BlockSpec(w_q.shape, lambda i: (0, 0)),
                  pl.BlockSpec(keys.shape, lambda i: (0, 0, 0))],
        out_specs=pl.BlockSpec((n_hp, nc, n_keys, LANES), lambda i: (0, i, 0, 0)),
        out_shape=jax.ShapeDtypeStruct((n_hp, m // LANES, n_keys, LANES), F32),
        compiler_params=_params("arbitrary"),
        name="peer_scores",
    )(h2, w_q, keys)


def _candidate_pairs(k):
    return [(x, y) for x in range(k) for y in range(k) if (x + 1) * (y + 1) <= k]


def _compare_exchange(xs, i, l):
    hi, lo = jnp.maximum(xs[i], xs[l]), jnp.minimum(xs[i], xs[l])
    xs[i], xs[l] = hi, lo


def _bitonic_merge_desc(xs):
    n = len(xs)
    j = n // 2
    while j >= 1:
        for i in range(n):
            if i ^ j > i:
                _compare_exchange(xs, i, i ^ j)
        j //= 2


def _sort_desc(xs):
    n = len(xs)
    k = 2
    while k <= n:
        j = k // 2
        while j >= 1:
            for i in range(n):
                l = i ^ j
                if l > i:
                    if (i & k) == 0 or k == n:
                        _compare_exchange(xs, i, l)
                    else:
                        _compare_exchange(xs, l, i)
            j //= 2
        k *= 2


def _top_sorted(s, k):
    xs = [s[v * SUBLANES:(v + 1) * SUBLANES, :] for v in range(s.shape[0] // SUBLANES)]
    assert len(xs) == k
    _sort_desc(xs)
    shift = SUBLANES // 2
    while shift >= 1:
        ys = [pltpu.roll(x, shift, axis=0) for x in xs]
        xs = [jnp.maximum(xs[v], ys[k - 1 - v]) for v in range(k)]
        _bitonic_merge_desc(xs)
        shift //= 2
    return xs


def _route_combo(sc_ref, slot, idx, t1_scr, e1_scr, e2_scr, cand_scr, nc):
    k = PEER_TOPK
    pairs = _candidate_pairs(k)
    h = idx // nc
    c = idx % nc
    s1 = sc_ref[2 * h, c]
    s2 = sc_ref[2 * h + 1, c]
    a = _top_sorted(s1, k)
    b = _top_sorted(s2, k)
    cand_scr[...] = jnp.full(cand_scr.shape, -jnp.inf, F32)
    for r, (x, y) in enumerate(pairs):
        cand_scr[r:r + 1, :] = (a[x] + b[y])[0:1, :]
    cand = cand_scr[...]
    work = cand
    remaining = jnp.full((1, LANES), float(k), F32)
    tau = jnp.full((1, LANES), -jnp.inf, F32)
    for _ in range(k):
        m = jnp.max(work, axis=0, keepdims=True)
        eq = work == m
        after = remaining - jnp.sum(jnp.where(eq, 1.0, 0.0), axis=0, keepdims=True)
        tau = jnp.where((remaining > 0.0) & (after <= 0.0), m, tau)
        remaining = after
        work = jnp.where(eq, -jnp.inf, work)
    top = (a[0] + b[0])[0:1, :]
    z = jnp.sum(jnp.where(cand >= tau, jnp.exp(cand - top), 0.0), axis=0, keepdims=True)
    tau_b = jnp.broadcast_to(tau, (SUBLANES, LANES))
    t1 = jnp.full(s1.shape, jnp.inf, F32)
    for x in reversed(range(k)):
        t1x = jnp.full((SUBLANES, LANES), jnp.inf, F32)
        for y in range(k // (x + 1)):
            t1x = jnp.where(a[x] + b[y] >= tau_b, b[y], t1x)
        t1 = jnp.where(s1 == a[x][0:1, :], t1x[0:1, :], t1)
    t1_scr[slot, h, c] = t1
    e1_scr[slot, h, c] = jnp.exp(s1 - a[0][0:1, :]) / z
    e2_scr[slot, h, c] = jnp.exp(s2 - b[0][0:1, :])


def _peer_dense_kernel(n_heads, ne, ht_ref, u_ref, vt_ref, sc_ref, scn_ref, ot_ref,
                       t1_scr, e1_scr, e2_scr, cand_scr, at_scr, g_scr):
    s = pl.program_id(0)
    et = u_ref.shape[0]
    tm = ht_ref.shape[1]
    nc = tm // LANES
    n_keys = sc_ref.shape[2]
    n_combos = n_heads * nc
    g_step = jnp.maximum(s - 1, 0)
    tile = g_step // ne
    e = g_step % ne
    slot = tile % 2
    at_w = s % 2
    at_r = 1 - at_w

    @pl.when(s == 0)
    def _():
        at_scr[...] = jnp.zeros(at_scr.shape, F32)

        def first_tile(idx, carry):
            _route_combo(sc_ref, 0, idx, t1_scr, e1_scr, e2_scr, cand_scr, nc)
            return carry

        lax.fori_loop(0, n_combos, first_tile, 0)

    @pl.when(e == 0)
    def _():
        ot_ref[...] = jnp.zeros(ot_ref.shape, F32)

    at_scr[at_w] = jnp.dot(u_ref[...], ht_ref[...], preferred_element_type=F32)

    per_step = -(-n_combos // ne)
    for r in range(per_step):
        idx = jnp.minimum(e * per_step + r, n_combos - 1)
        _route_combo(scn_ref, 1 - slot, idx, t1_scr, e1_scr, e2_scr, cand_scr, nc)

    half = tm // 2
    for n in range(2):
        for cc in range(half // LANES):
            c = n * (half // LANES) + cc
            for ib in range(et // n_keys):
                i = e * (et // n_keys) + ib
                w = jnp.zeros((n_keys, LANES), F32)
                for h in range(n_heads):
                    t1 = t1_scr[slot, h, c, pl.ds(i, 1), :]
                    e1 = e1_scr[slot, h, c, pl.ds(i, 1), :]
                    w = w + jnp.where(sc_ref[2 * h + 1, c] >= t1, e2_scr[slot, h, c], 0.0) * e1
                a_blk = at_scr[at_r, ib * n_keys:(ib + 1) * n_keys, c * LANES:(c + 1) * LANES]
                g_scr[ib * n_keys:(ib + 1) * n_keys, c * LANES:(c + 1) * LANES] = (
                    _gelu_exact(a_blk) * w).astype(BF16)
        cols = slice(n * half, (n + 1) * half)
        ot_ref[:, cols] += jnp.dot(vt_ref[...], g_scr[:, cols], preferred_element_type=F32)


def _peer_dense(h2t, u_tab, vt_tab, scores, n_heads, tm=512, et=1024):
    d, m = h2t.shape
    n_exp = u_tab.shape[0]
    n_hp, _, n_keys, _ = scores.shape
    tm = min(tm, m)
    nc = tm // LANES
    ne = n_exp // et
    n_tiles = m // tm
    total = n_tiles * ne
    n_cand = -(-len(_candidate_pairs(PEER_TOPK)) // SUBLANES) * SUBLANES
    per_tok = pltpu.VMEM((2, n_heads, nc, n_keys, LANES), F32)
    act = lambda s: jnp.minimum(s, total - 1)
    gat = lambda s: jnp.maximum(s - 1, 0)
    once = dict(pipeline_mode=pl.Buffered(1))
    return pl.pallas_call(
        functools.partial(_peer_dense_kernel, n_heads, ne),
        grid=(total + 1,),
        in_specs=[pl.BlockSpec((d, tm), lambda s: (0, act(s) // ne)),
                  pl.BlockSpec((et, d), lambda s: (act(s) % ne, 0)),
                  pl.BlockSpec((d, et), lambda s: (0, gat(s) % ne)),
                  pl.BlockSpec((n_hp, nc, n_keys, LANES), lambda s: (0, gat(s) // ne, 0, 0), **once),
                  pl.BlockSpec((n_hp, nc, n_keys, LANES),
                               lambda s: (0, jnp.minimum(gat(s) // ne + 1, n_tiles - 1), 0, 0), **once)],
        out_specs=pl.BlockSpec((d, tm), lambda s: (0, gat(s) // ne), **once),
        out_shape=jax.ShapeDtypeStruct((d, m), F32),
        scratch_shapes=[per_tok, per_tok, per_tok,
                        pltpu.VMEM((n_cand, LANES), F32),
                        pltpu.VMEM((2, et, tm), F32), pltpu.VMEM((et, tm), BF16)],
        compiler_params=_params("arbitrary"),
        name="peer_dense",
    )(h2t, u_tab, vt_tab, scores, scores)


def _final_kernel(x_ref, g_ref, pt_ref, o_ref):
    o_ref[...] = x_ref[...] + g_ref[...] * pt_ref[...].T


def _final_residual(x3, g2, peer_t, tm=512):
    bsz, seq, d = x3.shape
    tm = min(tm, seq)
    nt = seq // tm
    row = pl.BlockSpec((None, tm, d), lambda b, i: (b, i, 0))
    return pl.pallas_call(
        _final_kernel,
        grid=(bsz, nt),
        in_specs=[row, pl.BlockSpec((None, 1, d), lambda b, i: (b, 0, 0)),
                  pl.BlockSpec((d, tm), lambda b, i: (0, b * nt + i))],
        out_specs=row,
        out_shape=jax.ShapeDtypeStruct((bsz, seq, d), F32),
        compiler_params=_params("arbitrary", "arbitrary"),
        name="final_residual",
    )(x3, g2.reshape(bsz, 1, d), peer_t)


def _layer(x, cond_in, l, w_ada, b_ada, norm1_w, w_in, b_forget, q_norm_w, k_norm_w,
           ssm_A_re, ssm_A_im, ssm_log_dt, ssm_B_re, ssm_B_im, ssm_C_re, ssm_C_im, ssm_D,
           w_glu, b_glu, w_ssm_up, w_attn_up, w_out, norm2_w, w_peer_q, peer_sub_keys, peer_u, peer_v):
    bsz, seq, d = x.shape
    tokens = bsz * seq
    n_heads = b_forget.shape[1]
    head_dim = q_norm_w.shape[1]
    assert head_dim == LANES
    attn_w = n_heads * head_dim
    ssm_w = w_glu.shape[1]
    peer_heads = peer_sub_keys.shape[1]

    mod = _adaln(cond_in, w_ada[l], b_ada[l])
    sh1, sc1, g1, sh2, sc2, g2 = jnp.split(mod, N_MOD, axis=-1)

    h = _norm_modulate(x, norm1_w[l], sh1, sc1).reshape(tokens, d)
    wi = w_in[l].astype(BF16)
    o_qk, o_v, o_f, o_u, o_g = 0, 2 * attn_w, 3 * attn_w, 3 * attn_w + n_heads, 3 * attn_w + n_heads + ssm_w
    qk_w = jnp.concatenate([jnp.tile(q_norm_w[l] * (head_dim ** -0.5 * LOG2E), n_heads),
                            jnp.tile(k_norm_w[l], n_heads)])
    qk = _project(h, wi[:, o_qk:o_v], BF16, _ep_head_rms, qk_w, tn=1024, name="proj_qk")
    vt = _project_transposed(h, wi[:, o_v:o_f].T, BF16, name="proj_v")
    u = _project(h, wi[:, o_u:o_g], F32, name="proj_u")
    gates = _project(h, wi[:, o_g:], BF16, _ep_sigmoid, tn=1024, name="proj_gates")
    w_f = jnp.zeros((d, LANES), BF16).at[:, :n_heads].set(wi[:, o_f:o_u])
    b_f = jnp.zeros((LANES,), F32).at[:n_heads].set(b_forget[l])
    log_f = _project(h, w_f, F32, _ep_log_sigmoid, b_f, name="proj_forget")
    log_f_t = log_f.reshape(bsz, seq, LANES)[:, :, :SUBLANES].transpose(0, 2, 1)
    terms = _forget_bias_terms(log_f_t)[:, :, :n_heads]
    bias = jnp.zeros((tokens, LANES), BF16).at[:, :BIAS_TERMS * n_heads].set(
        terms.transpose(0, 3, 2, 1).reshape(tokens, BIAS_TERMS * n_heads).astype(BF16))
    y_attn = _fox_attention(qk, vt, bias, n_heads, bsz, seq)

    coef_re, coef_im, bb_re, bb_im = _ssm_prep(ssm_A_re[l], ssm_A_im[l], ssm_log_dt[l], ssm_B_re[l], ssm_B_im[l])
    y_ssm = _s5_glu(u.reshape(bsz, seq, ssm_w), coef_re, coef_im, bb_re, bb_im, ssm_C_re[l], ssm_C_im[l],
                    ssm_D[l], w_glu[l], b_glu[l]).reshape(tokens, ssm_w)

    merged = _merge(y_ssm, y_attn, gates, w_ssm_up[l].astype(BF16), w_attn_up[l].astype(BF16))
    x1, h2, h2t = _out_proj(merged.reshape(bsz, seq, d), w_out[l].astype(BF16), x, g1, norm2_w[l], sh2, sc2)

    keys = peer_sub_keys[l].reshape(2 * peer_heads, peer_sub_keys.shape[3], peer_sub_keys.shape[4]).astype(BF16)
    h2f = h2.reshape(tokens, d)
    scores = _peer_scores(h2f, w_peer_q[l].astype(BF16), keys)
    peer_t = _peer_dense(h2t, peer_u[l].astype(BF16), peer_v[l].T.astype(BF16), scores, peer_heads)
    return _final_residual(x1, g2, peer_t)


def kernel(x, c, w_ada, b_ada, norm1_w, w_in, b_forget, q_norm_w, k_norm_w, ssm_A_re, ssm_A_im, ssm_log_dt,
           ssm_B_re, ssm_B_im, ssm_C_re, ssm_C_im, ssm_D, w_glu, b_glu, w_ssm_up, w_attn_up, w_out, norm2_w,
           w_peer_q, peer_sub_keys, peer_u, peer_v):
    for l in range(w_ada.shape[0]):
        x = _layer(x, c, l, w_ada, b_ada, norm1_w, w_in, b_forget, q_norm_w, k_norm_w,
                   ssm_A_re, ssm_A_im, ssm_log_dt, ssm_B_re, ssm_B_im, ssm_C_re, ssm_C_im, ssm_D,
                   w_glu, b_glu, w_ssm_up, w_attn_up, w_out, norm2_w, w_peer_q, peer_sub_keys, peer_u, peer_v)
    return x
```

```python
import functools
import math

import numpy as np
import jax
import jax.numpy as jnp
from jax import lax
from jax.experimental import pallas as pl
from jax.experimental.pallas import tpu as pltpu

F32 = jnp.float32
BF16 = jnp.bfloat16

RMS_EPS = 1e-6
MASK_VALUE = -1e30
N_MOD = 6
PEER_TOPK = 16
SQRT_HALF = math.sqrt(0.5)
LOG2E = math.log2(math.e)
BIAS_TERMS = 3
DENOM_ROWS = 16

LANES = 128
SUBLANES = 8
MXU_DIM = 256
VMEM_LIMIT_BYTES = 56 * 1024 * 1024


def _params(*sem):
    return pltpu.CompilerParams(dimension_semantics=sem, vmem_limit_bytes=VMEM_LIMIT_BYTES)


def _gelu_exact(x):
    return 0.5 * x * (1.0 + lax.erf(x * SQRT_HALF))


def _adaln_kernel(c_ref, w_ref, b_ref, o_ref):
    c = c_ref[...]
    cond = c * jax.nn.sigmoid(c)
    o_ref[...] = jnp.dot(cond.astype(BF16), w_ref[...].astype(BF16), preferred_element_type=F32) + b_ref[...]


def _adaln(c, w, b, tn=512):
    bsz, d = c.shape
    n = w.shape[1]
    c8 = jnp.zeros((SUBLANES, d), F32).at[:bsz].set(c)
    out = pl.pallas_call(
        _adaln_kernel,
        grid=(n // tn,),
        in_specs=[pl.BlockSpec((SUBLANES, d), lambda j: (0, 0)),
                  pl.BlockSpec((d, tn), lambda j: (0, j)),
                  pl.BlockSpec((1, tn), lambda j: (0, j))],
        out_specs=pl.BlockSpec((SUBLANES, tn), lambda j: (0, j)),
        out_shape=jax.ShapeDtypeStruct((SUBLANES, n), F32),
        compiler_params=_params("arbitrary"),
        name="adaln",
    )(c8, w, b.reshape(1, n))
    return out[:bsz]


def _norm_mod(x, w, shift, scale):
    xf = x * lax.rsqrt(jnp.mean(x * x, axis=-1, keepdims=True) + RMS_EPS)
    return (xf * w) * (1.0 + scale) + shift


def _norm_mod_kernel(x_ref, w_ref, sh_ref, sc_ref, o_ref):
    o_ref[...] = _norm_mod(x_ref[...], w_ref[...], sh_ref[...], sc_ref[...]).astype(o_ref.dtype)


def _norm_modulate(x3, w, shift, scale, tm=512):
    bsz, seq, d = x3.shape
    tm = min(tm, seq)
    vec = pl.BlockSpec((None, 1, d), lambda b, i: (b, 0, 0))
    return pl.pallas_call(
        _norm_mod_kernel,
        grid=(bsz, seq // tm),
        in_specs=[pl.BlockSpec((None, tm, d), lambda b, i: (b, i, 0)),
                  pl.BlockSpec((1, d), lambda b, i: (0, 0)), vec, vec],
        out_specs=pl.BlockSpec((None, tm, d), lambda b, i: (b, i, 0)),
        out_shape=jax.ShapeDtypeStruct((bsz, seq, d), BF16),
        compiler_params=_params("arbitrary", "arbitrary"),
        name="norm_modulate",
    )(x3, w.reshape(1, d), shift.reshape(bsz, 1, d), scale.reshape(bsz, 1, d))


def _ep_plain(acc):
    return acc


def _ep_sigmoid(acc):
    return jax.nn.sigmoid(acc)


def _ep_head_rms(acc, w):
    outs = []
    for s in range(0, acc.shape[1], LANES):
        t = acc[:, s:s + LANES]
        t = t * lax.rsqrt(jnp.mean(t * t, axis=-1, keepdims=True) + RMS_EPS)
        outs.append(t * w[:, s:s + LANES])
    return jnp.concatenate(outs, axis=1)


def _ep_log_sigmoid(acc, b):
    z = acc + b
    return jnp.minimum(z, 0.0) - jnp.log1p(jnp.exp(-jnp.abs(z)))


def _proj_kernel(epilogue, a_ref, w_ref, *rest):
    o_ref = rest[-1]
    acc = jnp.dot(a_ref[...], w_ref[...], preferred_element_type=F32)
    o_ref[...] = epilogue(acc, *[r[...] for r in rest[:-1]]).astype(o_ref.dtype)


def _project(a, w, out_dtype, epilogue=_ep_plain, row_vec=None, tm=1024, tn=512, name="project"):
    m, k = a.shape
    n = w.shape[1]
    tm, tn = min(tm, m), min(tn, n)
    in_specs = [pl.BlockSpec((tm, k), lambda i, j: (i, 0)), pl.BlockSpec((k, tn), lambda i, j: (0, j))]
    args = [a, w]
    if row_vec is not None:
        in_specs.append(pl.BlockSpec((1, tn), lambda i, j: (0, j)))
        args.append(row_vec.reshape(1, n).astype(F32))
    return pl.pallas_call(
        functools.partial(_proj_kernel, epilogue),
        grid=(m // tm, n // tn),
        in_specs=in_specs,
        out_specs=pl.BlockSpec((tm, tn), lambda i, j: (i, j)),
        out_shape=jax.ShapeDtypeStruct((m, n), out_dtype),
        compiler_params=_params("arbitrary", "arbitrary"),
        name=name,
    )(*args)


def _proj_t_kernel(a_ref, wt_ref, o_ref):
    o_ref[...] = lax.dot_general(wt_ref[...], a_ref[...], (((1,), (1,)), ((), ())),
                                 preferred_element_type=F32).astype(o_ref.dtype)


def _project_transposed(a, wt, out_dtype, tm=1024, tn=512, name="project_t"):
    m, k = a.shape
    n = wt.shape[0]
    tm, tn = min(tm, m), min(tn, n)
    return pl.pallas_call(
        _proj_t_kernel,
        grid=(m // tm, n // tn),
        in_specs=[pl.BlockSpec((tm, k), lambda i, j: (i, 0)), pl.BlockSpec((tn, k), lambda i, j: (j, 0))],
        out_specs=pl.BlockSpec((tn, tm), lambda i, j: (j, i)),
        out_shape=jax.ShapeDtypeStruct((n, m), out_dtype),
        compiler_params=_params("arbitrary", "arbitrary"),
        name=name,
    )(a, wt)


def _cumsum_kernel(x_ref, o_ref):
    x = x_ref[...]
    n = x.shape[1]
    lane = lax.broadcasted_iota(jnp.int32, x.shape, 1)
    k = 1
    while k < n:
        x = x + jnp.where(lane >= k, pltpu.roll(x, k, axis=1), 0.0)
        k *= 2
    g = -LOG2E * x
    hi = g.astype(BF16).astype(F32)
    mid = (g - hi).astype(BF16).astype(F32)
    o_ref[0] = hi
    o_ref[1] = mid
    o_ref[2] = (g - hi - mid).astype(BF16).astype(F32)


def _forget_bias_terms(x):
    bsz, r, n = x.shape
    return pl.pallas_call(
        _cumsum_kernel,
        grid=(bsz,),
        in_specs=[pl.BlockSpec((None, r, n), lambda b: (b, 0, 0))],
        out_specs=pl.BlockSpec((None, BIAS_TERMS, r, n), lambda b: (b, 0, 0, 0)),
        out_shape=jax.ShapeDtypeStruct((bsz, BIAS_TERMS, r, n), F32),
        compiler_params=_params("arbitrary"),
        name="forget_cumsum",
    )(x)


def _attn_kernel(n_heads, qi_ref, kj_ref, q_ref, k_ref, vt_ref, bias_ref, o_ref, m_scr, acc_scr):
    step = pl.program_id(1)
    i = qi_ref[step]
    j = kj_ref[step]
    tq = q_ref.shape[0]
    tk = k_ref.shape[0]
    ratio = tq // tk

    @pl.when(j == 0)
    def _():
        m_scr[...] = jnp.full(m_scr.shape, MASK_VALUE, F32)
        acc_scr[...] = jnp.zeros(acc_scr.shape, F32)

    def update(masked):
        bias = bias_ref[...]
        ones_rows = jnp.where(lax.broadcasted_iota(jnp.int32, (DENOM_ROWS, tk), 0) == 0, 1.0, 0.0).astype(BF16)
        lane = lax.broadcasted_iota(jnp.int32, (tq, LANES), 1)
        if masked:
            key = j * tk + lax.broadcasted_iota(jnp.int32, (tk, tq), 0)
            qry = i * tq + lax.broadcasted_iota(jnp.int32, (tk, tq), 1)
            keep = key <= qry
        def scores(h):
            sl = slice(h * LANES, (h + 1) * LANES)
            pick = jnp.where((lane >= BIAS_TERMS * h) & (lane < BIAS_TERMS * (h + 1)), 1.0, 0.0).astype(BF16)
            qq = jnp.concatenate([q_ref[:, sl], pick], axis=1)
            kk = jnp.concatenate([k_ref[:, sl], bias], axis=1)
            return lax.dot_general(kk, qq, (((1,), (1,)), ((), ())), preferred_element_type=F32)

        st_next = scores(0)
        for h in range(n_heads):
            sl = slice(h * LANES, (h + 1) * LANES)
            st = st_next
            if h + 1 < n_heads:
                st_next = scores(h + 1)
            if masked:
                st = jnp.where(keep, st, MASK_VALUE)
            m_prev = m_scr[h]
            m_new = jnp.maximum(m_prev, jnp.max(st, axis=0, keepdims=True))
            alpha = jnp.exp2(m_prev - m_new)
            p = jnp.exp2(st - m_new)
            v_ext = jnp.concatenate([vt_ref[sl, :], ones_rows], axis=0)
            acc_scr[h] = alpha * acc_scr[h] + jnp.dot(v_ext, p.astype(BF16), preferred_element_type=F32)
            m_scr[h] = m_new

    @pl.when(j < i * ratio)
    def _():
        update(False)

    @pl.when(j >= i * ratio)
    def _():
        update(True)

    @pl.when(j == (i + 1) * ratio - 1)
    def _():
        for h in range(n_heads):
            acc = acc_scr[h]
            o_ref[:, h * LANES:(h + 1) * LANES] = (acc[:LANES] / acc[LANES:LANES + 1]).T.astype(o_ref.dtype)


def _fox_attention(qk, vt, bias, n_heads, bsz, seq, tq=1024, tk=512):
    tq, tk = min(tq, seq), min(tk, seq)
    nq, nk = seq // tq, seq // tk
    ratio = tq // tk
    width = n_heads * LANES
    pairs = [(i, j) for i in range(nq) for j in range((i + 1) * ratio)]
    qi = jnp.asarray(np.array([p[0] for p in pairs], np.int32))
    kj = jnp.asarray(np.array([p[1] for p in pairs], np.int32))
    grid_spec = pltpu.PrefetchScalarGridSpec(
        num_scalar_prefetch=2,
        grid=(bsz, len(pairs)),
        in_specs=[
            pl.BlockSpec((tq, width), lambda b, s, qi, kj: (b * nq + qi[s], 0)),
            pl.BlockSpec((tk, width), lambda b, s, qi, kj: (b * nk + kj[s], 1)),
            pl.BlockSpec((width, tk), lambda b, s, qi, kj: (0, b * nk + kj[s])),
            pl.BlockSpec((tk, LANES), lambda b, s, qi, kj: (b * nk + kj[s], 0)),
        ],
        out_specs=pl.BlockSpec((tq, width), lambda b, s, qi, kj: (b * nq + qi[s], 0)),
        scratch_shapes=[pltpu.VMEM((n_heads, 1, tq), F32), pltpu.VMEM((n_heads, LANES + DENOM_ROWS, tq), F32)],
    )
    return pl.pallas_call(
        functools.partial(_attn_kernel, n_heads),
        grid_spec=grid_spec,
        out_shape=jax.ShapeDtypeStruct((bsz * seq, width), BF16),
        compiler_params=_params("arbitrary", "arbitrary"),
        name="fox_attention",
    )(qi, kj, qk, qk, vt, bias)


def _ssm_prep_kernel(are_ref, aim_ref, ldt_ref, bre_ref, bim_ref, cre_ref, cim_ref, bbre_ref, bbim_ref):
    lam_re = are_ref[...]
    lam_im = aim_ref[...]
    dt = jnp.exp(ldt_ref[...])
    mag = jnp.exp(lam_re * dt)
    ab_re = mag * jnp.cos(lam_im * dt)
    ab_im = mag * jnp.sin(lam_im * dt)
    den = lam_re * lam_re + lam_im * lam_im
    nr = ab_re - 1.0
    coef_re = (nr * lam_re + ab_im * lam_im) / den
    coef_im = (ab_im * lam_re - nr * lam_im) / den
    b_re = bre_ref[...]
    b_im = bim_ref[...]
    bbre_ref[...] = coef_re * b_re - coef_im * b_im
    bbim_ref[...] = coef_re * b_im + coef_im * b_re
    pr, pi = ab_re, ab_im
    pows = []
    for _ in range(SUBLANES):
        pows.append((pr, pi))
        pr, pi = pr * ab_re - pi * ab_im, pr * ab_im + pi * ab_re
    row = lax.broadcasted_iota(jnp.int32, (SUBLANES, lam_re.shape[1]), 0)
    for idx, k in enumerate((1, 2, 4)):
        cre_ref[idx] = jnp.where(row >= k, pows[k - 1][0], 0.0)
        cim_ref[idx] = jnp.where(row >= k, pows[k - 1][1], 0.0)
    cre_ref[3] = jnp.concatenate([p[0] for p in pows], axis=0)
    cim_ref[3] = jnp.concatenate([p[1] for p in pows], axis=0)


def _ssm_prep(a_re, a_im, log_dt, b_re, b_im):
    g, p = a_re.shape
    gc = b_re.shape[2]
    gp = g * p
    flat = lambda a: a.reshape(1, gp)
    ldt = jnp.broadcast_to(log_dt[:, None], (g, p)).reshape(1, gp)
    bt = lambda b: b.transpose(2, 0, 1).reshape(gc, gp)
    return pl.pallas_call(
        _ssm_prep_kernel,
        out_shape=[jax.ShapeDtypeStruct((4, SUBLANES, gp), F32)] * 2 + [jax.ShapeDtypeStruct((gc, gp), F32)] * 2,
        name="ssm_discretize",
    )(flat(a_re), flat(a_im), ldt, bt(b_re), bt(b_im))


def _ssm_kernel(lane_chunk, u_ref, bre_ref, bim_ref, cfre_ref, cfim_ref, ctre_ref, ctim_ref, d_ref,
                wg_ref, bg_ref, o_ref, xre_scr, xim_scr, car_scr, cai_scr):
    tt = u_ref.shape[0]
    gp = xre_scr.shape[1]

    @pl.when(pl.program_id(1) == 0)
    def _():
        car_scr[...] = jnp.zeros(car_scr.shape, F32)
        cai_scr[...] = jnp.zeros(cai_scr.shape, F32)

    u = u_ref[...]
    ub = u.astype(BF16)
    n_chunks, cw, sw = bre_ref.shape
    for k in range(n_chunks):
        uk = ub[:, k * cw:(k + 1) * cw]
        xre_scr[:, k * sw:(k + 1) * sw] = jnp.dot(uk, bre_ref[k], preferred_element_type=F32)
        xim_scr[:, k * sw:(k + 1) * sw] = jnp.dot(uk, bim_ref[k], preferred_element_type=F32)

    for c0 in range(0, gp, lane_chunk):
        cols = slice(c0, c0 + lane_chunk)

        def body(r, carry, cols=cols):
            cr, ci = carry
            coefs = [(cfre_ref[n, :, cols], cfim_ref[n, :, cols]) for n in range(4)]
            rows = pl.ds(pl.multiple_of(r * SUBLANES, SUBLANES), SUBLANES)
            xr = xre_scr[rows, cols]
            xi = xim_scr[rows, cols]
            for n, k in enumerate((1, 2, 4)):
                ar, ai = coefs[n]
                sr = pltpu.roll(xr, k, axis=0)
                si = pltpu.roll(xi, k, axis=0)
                xr, xi = xr + ar * sr - ai * si, xi + ar * si + ai * sr
            pr, pi = coefs[3]
            xr, xi = xr + pr * cr - pi * ci, xi + pr * ci + pi * cr
            xre_scr[rows, cols] = xr
            xim_scr[rows, cols] = xi
            last = slice(SUBLANES - 1, SUBLANES)
            return (jnp.broadcast_to(xr[last, :], xr.shape), jnp.broadcast_to(xi[last, :], xi.shape))

        cr, ci = lax.fori_loop(0, tt // SUBLANES, body, (car_scr[:, cols], cai_scr[:, cols]))
        car_scr[:, cols] = cr
        cai_scr[:, cols] = ci

    cx = jnp.concatenate(
        [jnp.dot(xre_scr[:, k * sw:(k + 1) * sw].astype(BF16), ctre_ref[k], preferred_element_type=F32)
         - jnp.dot(xim_scr[:, k * sw:(k + 1) * sw].astype(BF16), ctim_ref[k], preferred_element_type=F32)
         for k in range(n_chunks)], axis=1)
    y = cx + d_ref[...] * u
    g = _gelu_exact(y)
    gate = jnp.dot(g.astype(BF16), wg_ref[...], preferred_element_type=F32) + bg_ref[...]
    o_ref[...] = (g * jax.nn.sigmoid(gate)).astype(o_ref.dtype)


def _block_diag_rows(m, groups):
    gc, gp = m.shape
    p = gp // groups
    rows = jnp.tile(m, (groups, 1))
    rg = lax.broadcasted_iota(jnp.int32, (groups * gc, gp), 0) // gc
    cg = lax.broadcasted_iota(jnp.int32, (groups * gc, gp), 1) // p
    return jnp.where(rg == cg, rows, 0.0)


def _s5_glu(u3, coef_re, coef_im, bb_re, bb_im, c_re, c_im, d_skip, w_glu, b_glu, tt=512, lane_chunk=512):
    bsz, seq, width = u3.shape
    g, gc, p = c_re.shape
    gp = g * p
    tt = min(tt, seq)
    lane_chunk = min(lane_chunk, gp)
    cw = min(MXU_DIM, width)
    n_chunks = width // cw
    sw = gp // n_chunks

    def diag_chunks(m):
        return jnp.stack([m[k * cw:(k + 1) * cw, k * sw:(k + 1) * sw] for k in range(n_chunks)])

    bd_b_re = diag_chunks(_block_diag_rows(bb_re, g)).astype(BF16)
    bd_b_im = diag_chunks(_block_diag_rows(bb_im, g)).astype(BF16)
    ct = lambda c: diag_chunks(_block_diag_rows(c.transpose(1, 0, 2).reshape(gc, gp), g)
                               ).transpose(0, 2, 1).astype(BF16)
    const = lambda shape: pl.BlockSpec(shape, lambda b, i: tuple(0 for _ in shape))
    return pl.pallas_call(
        functools.partial(_ssm_kernel, lane_chunk),
        grid=(bsz, seq // tt),
        in_specs=[pl.BlockSpec((None, tt, width), lambda b, i: (b, i, 0)),
                  const((n_chunks, cw, sw)), const((n_chunks, cw, sw)),
                  const((4, SUBLANES, gp)), const((4, SUBLANES, gp)),
                  const((n_chunks, sw, cw)), const((n_chunks, sw, cw)),
                  const((1, width)), const((width, width)), const((1, width))],
        out_specs=pl.BlockSpec((None, tt, width), lambda b, i: (b, i, 0)),
        out_shape=jax.ShapeDtypeStruct((bsz, seq, width), BF16),
        scratch_shapes=[pltpu.VMEM((tt, gp), F32), pltpu.VMEM((tt, gp), F32),
                        pltpu.VMEM((SUBLANES, gp), F32), pltpu.VMEM((SUBLANES, gp), F32)],
        compiler_params=_params("arbitrary", "arbitrary"),
        name="s5_glu",
    )(u3, bd_b_re, bd_b_im, coef_re, coef_im, ct(c_re), ct(c_im),
      d_skip.reshape(1, width).astype(F32), w_glu.astype(BF16), b_glu.reshape(1, width).astype(F32))


def _merge_kernel(ys_ref, ya_ref, gt_ref, wsu_ref, wau_ref, o_ref):
    d = o_ref.shape[1]
    ms = jnp.dot(ys_ref[...], wsu_ref[...], preferred_element_type=F32)
    ma = jnp.dot(ya_ref[...], wau_ref[...], preferred_element_type=F32)
    o_ref[...] = (gt_ref[:, :d].astype(F32) * ms + gt_ref[:, d:].astype(F32) * ma).astype(o_ref.dtype)


def _merge(ys, ya, gates, w_ssm_up, w_attn_up, tm=512):
    m = ys.shape[0]
    d = w_ssm_up.shape[1]
    tm = min(tm, m)
    row = lambda w: pl.BlockSpec((tm, w), lambda i: (i, 0))
    const = lambda a: pl.BlockSpec(a.shape, lambda i: (0, 0))
    return pl.pallas_call(
        _merge_kernel,
        grid=(m // tm,),
        in_specs=[row(ys.shape[1]), row(ya.shape[1]), row(gates.shape[1]), const(w_ssm_up), const(w_attn_up)],
        out_specs=row(d),
        out_shape=jax.ShapeDtypeStruct((m, d), BF16),
        compiler_params=_params("arbitrary"),
        name="gated_merge",
    )(ys, ya, gates, w_ssm_up, w_attn_up)


def _out_proj_kernel(mg_ref, w_ref, x_ref, g1_ref, nw_ref, sh_ref, sc_ref, x1_ref, h2_ref, h2t_ref):
    x1 = x_ref[...] + g1_ref[...] * jnp.dot(mg_ref[...], w_ref[...], preferred_element_type=F32)
    x1_ref[...] = x1
    h2 = _norm_mod(x1, nw_ref[...], sh_ref[...], sc_ref[...])
    h2_ref[...] = h2.astype(h2_ref.dtype)
    h2t_ref[...] = h2.T.astype(h2t_ref.dtype)


def _out_proj(merged3, w_out, x3, g1, norm_w, shift, scale, tm=512):
    bsz, seq, d = x3.shape
    tm = min(tm, seq)
    nt = seq // tm
    row = pl.BlockSpec((None, tm, d), lambda b, i: (b, i, 0))
    vec = pl.BlockSpec((None, 1, d), lambda b, i: (b, 0, 0))
    r3 = lambda a: a.reshape(bsz, 1, d)
    return pl.pallas_call(
        _out_proj_kernel,
        grid=(bsz, nt),
        in_specs=[row, pl.BlockSpec((d, d), lambda b, i: (0, 0)), row, vec,
                  pl.BlockSpec((1, d), lambda b, i: (0, 0)), vec, vec],
        out_specs=[row, row, pl.BlockSpec((d, tm), lambda b, i: (0, b * nt + i))],
        out_shape=[jax.ShapeDtypeStruct((bsz, seq, d), F32), jax.ShapeDtypeStruct((bsz, seq, d), BF16),
                   jax.ShapeDtypeStruct((d, bsz * seq), BF16)],
        compiler_params=_params("arbitrary", "arbitrary"),
        name="out_proj_residual",
    )(merged3, w_out, x3, r3(g1), norm_w.reshape(1, d), r3(shift), r3(scale))


def _peer_scores_kernel(h_ref, wq_ref, keys_ref, o_ref):
    q = jnp.dot(h_ref[...], wq_ref[...], preferred_element_type=F32).astype(BF16)
    n_hp = keys_ref.shape[0]
    nc = o_ref.shape[1]
    for hp in range(n_hp):
        dk = keys_ref.shape[2]
        st = lax.dot_general(keys_ref[hp], q[:, hp * dk:(hp + 1) * dk], (((1,), (1,)), ((), ())),
                             preferred_element_type=F32)
        for c in range(nc):
            o_ref[hp, c] = st[:, c * LANES:(c + 1) * LANES]


def _peer_scores(h2, w_q, keys, tm=512):
    m, d = h2.shape
    n_hp, n_keys, dk = keys.shape
    tm = min(tm, m)
    nc = tm // LANES
    return pl.pallas_call(
        _peer_scores_kernel,
        grid=(m // tm,),
        in_specs=[pl.BlockSpec((tm, d), lambda i: (i, 0)),
                  pl.BlockSpec(w_q.shape, lambda i: (0, 0)),
                  pl.BlockSpec(keys.shape, lambda i: (0, 0, 0))],
        out_specs=pl.BlockSpec((n_hp, nc, n_keys, LANES), lambda i: (0, i, 0, 0)),
        out_shape=jax.ShapeDtypeStruct((n_hp, m // LANES, n_keys, LANES), F32),
        compiler_params=_params("arbitrary"),
        name="peer_scores",
    )(h2, w_q, keys)


def _candidate_pairs(k):
    return [(x, y) for x in range(k) for y in range(k) if (x + 1) * (y + 1) <= k]


def _compare_exchange(xs, i, l):
    hi, lo = jnp.maximum(xs[i], xs[l]), jnp.minimum(xs[i], xs[l])
    xs[i], xs[l] = hi, lo


def _bitonic_merge_desc(xs):
    n = len(xs)
    j = n // 2
    while j >= 1:
        for i in range(n):
            if i ^ j > i:
                _compare_exchange(xs, i, i ^ j)
        j //= 2


def _sort_desc(xs):
    n = len(xs)
    k = 2
    while k <= n:
        j = k // 2
        while j >= 1:
            for i in range(n):
                l = i ^ j
                if l > i:
                    if (i & k) == 0 or k == n:
                        _compare_exchange(xs, i, l)
                    else:
                        _compare_exchange(xs, l, i)
            j //= 2
        k *= 2


def _top_sorted(s, k):
    xs = [s[v * SUBLANES:(v + 1) * SUBLANES, :] for v in range(s.shape[0] // SUBLANES)]
    assert len(xs) == k
    _sort_desc(xs)
    shift = SUBLANES // 2
    while shift >= 1:
        ys = [pltpu.roll(x, shift, axis=0) for x in xs]
        xs = [jnp.maximum(xs[v], ys[k - 1 - v]) for v in range(k)]
        _bitonic_merge_desc(xs)
        shift //= 2
    return xs


def _route_combo(sc_ref, slot, idx, n1_scr, e1_scr, r2_scr, e2_scr, cand_scr, tmp_scr, nc):
    k = PEER_TOPK
    pairs = _candidate_pairs(k)
    h = idx // nc
    c = idx % nc
    s1 = sc_ref[2 * h, c]
    s2 = sc_ref[2 * h + 1, c]
    a = _top_sorted(s1, k)
    b = _top_sorted(s2, k)
    cand_scr[...] = jnp.full(cand_scr.shape, -jnp.inf, F32)
    for r, (x, y) in enumerate(pairs):
        cand_scr[r:r + 1, :] = (a[x] + b[y])[0:1, :]
    cand = cand_scr[...]
    work = cand
    remaining = jnp.full((1, LANES), float(k), F32)
    tau = jnp.full((1, LANES), -jnp.inf, F32)
    for _ in range(k):
        m = jnp.max(work, axis=0, keepdims=True)
        eq = work == m
        after = remaining - jnp.sum(jnp.where(eq, 1.0, 0.0), axis=0, keepdims=True)
        tau = jnp.where((remaining > 0.0) & (after <= 0.0), m, tau)
        remaining = after
        work = jnp.where(eq, -jnp.inf, work)
    top = (a[0] + b[0])[0:1, :]
    z = jnp.sum(jnp.where(cand >= tau, jnp.exp(cand - top), 0.0), axis=0, keepdims=True)
    tau_b = jnp.broadcast_to(tau, (SUBLANES, LANES))
    cnt = jnp.zeros(s1.shape, F32)
    rank = jnp.full(s2.shape, float(k), F32)
    for x in reversed(range(k)):
        nx = jnp.zeros((SUBLANES, LANES), F32)
        for y in range(k // (x + 1)):
            nx = nx + jnp.where(a[x] + b[y] >= tau_b, 1.0, 0.0)
        cnt = jnp.where(s1 == a[x][0:1, :], nx[0:1, :], cnt)
        rank = jnp.where(s2 == b[x][0:1, :], float(x), rank)
    n1_scr[slot, h, c] = cnt
    e1_scr[slot, h, c] = jnp.exp(s1 - a[0][0:1, :]) / z
    tmp_scr[0] = rank
    tmp_scr[1] = jnp.exp(s2 - b[0][0:1, :])
    r2_scr[slot, h, c] = tmp_scr[0].astype(BF16)
    e2_scr[slot, h, c] = tmp_scr[1].astype(BF16)


def _peer_dense_kernel(n_heads, ne, ht_ref, u_ref, vt_ref, scn_ref, ot_ref,
                       n1_scr, e1_scr, r2_scr, e2_scr, cand_scr, tmp_scr, at_scr, g_scr):
    s = pl.program_id(0)
    et = u_ref.shape[0]
    tm = ht_ref.shape[1]
    nc = tm // LANES
    n_keys = scn_ref.shape[2]
    n_combos = n_heads * nc
    g_step = jnp.maximum(s - 1, 0)
    tile = g_step // ne
    e = g_step % ne
    slot = tile % 2
    at_w = s % 2
    at_r = 1 - at_w

    @pl.when(s == 0)
    def _():
        at_scr[...] = jnp.zeros(at_scr.shape, F32)

        def first_tile(idx, carry):
            _route_combo(scn_ref, 0, idx, n1_scr, e1_scr, r2_scr, e2_scr, cand_scr, tmp_scr, nc)
            return carry

        lax.fori_loop(0, n_combos, first_tile, 0)

    @pl.when(e == 0)
    def _():
        ot_ref[...] = jnp.zeros(ot_ref.shape, F32)

    at_scr[at_w] = jnp.dot(u_ref[...], ht_ref[...], preferred_element_type=F32)

    per_step = -(-n_combos // ne)
    for r in range(per_step):
        idx = jnp.minimum(e * per_step + r, n_combos - 1)
        _route_combo(scn_ref, 1 - slot, idx, n1_scr, e1_scr, r2_scr, e2_scr, cand_scr, tmp_scr, nc)

    half = tm // 2
    for n in range(2):
        for cc in range(half // LANES):
            c = n * (half // LANES) + cc
            for ib in range(et // n_keys):
                i = e * (et // n_keys) + ib
                w = jnp.zeros((n_keys, LANES), BF16)
                for h in range(n_heads):
                    n1 = jnp.broadcast_to(n1_scr[slot, h, c, pl.ds(i, 1), :].astype(BF16), (n_keys, LANES))
                    e1 = jnp.broadcast_to(e1_scr[slot, h, c, pl.ds(i, 1), :].astype(BF16), (n_keys, LANES))
                    w = w + jnp.where(r2_scr[slot, h, c] < n1, e2_scr[slot, h, c], jnp.zeros((), BF16)) * e1
                a_blk = at_scr[at_r, ib * n_keys:(ib + 1) * n_keys, c * LANES:(c + 1) * LANES]
                g_scr[ib * n_keys:(ib + 1) * n_keys, c * LANES:(c + 1) * LANES] = (
                    _gelu_exact(a_blk).astype(BF16) * w)
        cols = slice(n * half, (n + 1) * half)
        ot_ref[:, cols] += jnp.dot(vt_ref[...], g_scr[:, cols], preferred_element_type=F32)


def _peer_dense(h2t, u_tab, vt_tab, scores, n_heads, tm=512, et=1024):
    d, m = h2t.shape
    n_exp = u_tab.shape[0]
    n_hp, _, n_keys, _ = scores.shape
    tm = min(tm, m)
    nc = tm // LANES
    ne = n_exp // et
    n_tiles = m // tm
    total = n_tiles * ne
    n_cand = -(-len(_candidate_pairs(PEER_TOPK)) // SUBLANES) * SUBLANES
    per_tok = pltpu.VMEM((2, n_heads, nc, n_keys, LANES), F32)
    per_tok_b = pltpu.VMEM((2, n_heads, nc, n_keys, LANES), BF16)
    act = lambda s: jnp.minimum(s, total - 1)
    gat = lambda s: jnp.maximum(s - 1, 0)
    nxt = lambda s: jnp.minimum(gat(s) // ne + jnp.minimum(s, 1), n_tiles - 1)
    once = dict(pipeline_mode=pl.Buffered(1))
    return pl.pallas_call(
        functools.partial(_peer_dense_kernel, n_heads, ne),
        grid=(total + 1,),
        in_specs=[pl.BlockSpec((d, tm), lambda s: (0, act(s) // ne)),
                  pl.BlockSpec((et, d), lambda s: (act(s) % ne, 0)),
                  pl.BlockSpec((d, et), lambda s: (0, gat(s) % ne)),
                  pl.BlockSpec((n_hp, nc, n_keys, LANES), lambda s: (0, nxt(s), 0, 0), **once)],
        out_specs=pl.BlockSpec((d, tm), lambda s: (0, gat(s) // ne), **once),
        out_shape=jax.ShapeDtypeStruct((d, m), F32),
        scratch_shapes=[per_tok, per_tok, per_tok_b, per_tok_b,
                        pltpu.VMEM((n_cand, LANES), F32), pltpu.VMEM((2, n_keys, LANES), F32),
                        pltpu.VMEM((2, et, tm), F32), pltpu.VMEM((et, tm), BF16)],
        compiler_params=_params("arbitrary"),
        name="peer_dense",
    )(h2t, u_tab, vt_tab, scores)


def _final_kernel(x_ref, g_ref, pt_ref, o_ref):
    o_ref[...] = x_ref[...] + g_ref[...] * pt_ref[...].T


def _final_residual(x3, g2, peer_t, tm=512):
    bsz, seq, d = x3.shape
    tm = min(tm, seq)
    nt = seq // tm
    row = pl.BlockSpec((None, tm, d), lambda b, i: (b, i, 0))
    return pl.pallas_call(
        _final_kernel,
        grid=(bsz, nt),
        in_specs=[row, pl.BlockSpec((None, 1, d), lambda b, i: (b, 0, 0)),
                  pl.BlockSpec((d, tm), lambda b, i: (0, b * nt + i))],
        out_specs=row,
        out_shape=jax.ShapeDtypeStruct((bsz, seq, d), F32),
        compiler_params=_params("arbitrary", "arbitrary"),
        name="final_residual",
    )(x3, g2.reshape(bsz, 1, d), peer_t)


def _layer(x, cond_in, l, w_ada, b_ada, norm1_w, w_in, b_forget, q_norm_w, k_norm_w,
           ssm_A_re, ssm_A_im, ssm_log_dt, ssm_B_re, ssm_B_im, ssm_C_re, ssm_C_im, ssm_D,
           w_glu, b_glu, w_ssm_up, w_attn_up, w_out, norm2_w, w_peer_q, peer_sub_keys, peer_u, peer_v):
    bsz, seq, d = x.shape
    tokens = bsz * seq
    n_heads = b_forget.shape[1]
    head_dim = q_norm_w.shape[1]
    assert head_dim == LANES
    attn_w = n_heads * head_dim
    ssm_w = w_glu.shape[1]
    peer_heads = peer_sub_keys.shape[1]

    mod = _adaln(cond_in, w_ada[l], b_ada[l])
    sh1, sc1, g1, sh2, sc2, g2 = jnp.split(mod, N_MOD, axis=-1)

    h = _norm_modulate(x, norm1_w[l], sh1, sc1).reshape(tokens, d)
    wi = w_in[l].astype(BF16)
    o_qk, o_v, o_f, o_u, o_g = 0, 2 * attn_w, 3 * attn_w, 3 * attn_w + n_heads, 3 * attn_w + n_heads + ssm_w
    qk_w = jnp.concatenate([jnp.tile(q_norm_w[l] * (head_dim ** -0.5 * LOG2E), n_heads),
                            jnp.tile(k_norm_w[l], n_heads)])
    qk = _project(h, wi[:, o_qk:o_v], BF16, _ep_head_rms, qk_w, tn=1024, name="proj_qk")
    vt = _project_transposed(h, wi[:, o_v:o_f].T, BF16, name="proj_v")
    u = _project(h, wi[:, o_u:o_g], F32, name="proj_u")
    gates = _project(h, wi[:, o_g:], BF16, _ep_sigmoid, tn=1024, name="proj_gates")
    w_f = jnp.zeros((d, LANES), BF16).at[:, :n_heads].set(wi[:, o_f:o_u])
    b_f = jnp.zeros((LANES,), F32).at[:n_heads].set(b_forget[l])
    log_f = _project(h, w_f, F32, _ep_log_sigmoid, b_f, name="proj_forget")
    log_f_t = log_f.reshape(bsz, seq, LANES)[:, :, :SUBLANES].transpose(0, 2, 1)
    terms = _forget_bias_terms(log_f_t)[:, :, :n_heads]
    bias = jnp.zeros((tokens, LANES), BF16).at[:, :BIAS_TERMS * n_heads].set(
        terms.transpose(0, 3, 2, 1).reshape(tokens, BIAS_TERMS * n_heads).astype(BF16))
    y_attn = _fox_attention(qk, vt, bias, n_heads, bsz, seq)

    coef_re, coef_im, bb_re, bb_im = _ssm_prep(ssm_A_re[l], ssm_A_im[l], ssm_log_dt[l], ssm_B_re[l], ssm_B_im[l])
    y_ssm = _s5_glu(u.reshape(bsz, seq, ssm_w), coef_re, coef_im, bb_re, bb_im, ssm_C_re[l], ssm_C_im[l],
                    ssm_D[l], w_glu[l], b_glu[l]).reshape(tokens, ssm_w)

    merged = _merge(y_ssm, y_attn, gates, w_ssm_up[l].astype(BF16), w_attn_up[l].astype(BF16))
    x1, h2, h2t = _out_proj(merged.reshape(bsz, seq, d), w_out[l].astype(BF16), x, g1, norm2_w[l], sh2, sc2)

    keys = peer_sub_keys[l].reshape(2 * peer_heads, peer_sub_keys.shape[3], peer_sub_keys.shape[4]).astype(BF16)
    h2f = h2.reshape(tokens, d)
    scores = _peer_scores(h2f, w_peer_q[l].astype(BF16), keys)
    peer_t = _peer_dense(h2t, peer_u[l].astype(BF16), peer_v[l].T.astype(BF16), scores, peer_heads)
    return _final_residual(x1, g2, peer_t)


def kernel(x, c, w_ada, b_ada, norm1_w, w_in, b_forget, q_norm_w, k_norm_w, ssm_A_re, ssm_A_im, ssm_log_dt,
           ssm_B_re, ssm_B_im, ssm_C_re, ssm_C_im, ssm_D, w_glu, b_glu, w_ssm_up, w_attn_up, w_out, norm2_w,
           w_peer_q, peer_sub_keys, peer_u, peer_v):
    for l in range(w_ada.shape[0]):
        x = _layer(x, c, l, w_ada, b_ada, norm1_w, w_in, b_forget, q_norm_w, k_norm_w,
                   ssm_A_re, ssm_A_im, ssm_log_dt, ssm_B_re, ssm_B_im, ssm_C_re, ssm_C_im, ssm_D,
                   w_glu, b_glu, w_ssm_up, w_attn_up, w_out, norm2_w, w_peer_q, peer_sub_keys, peer_u, peer_v)
    return x
```
